```python
import jax, jax.numpy as jnp
from jax import lax
import numpy as np

D_MODEL = 1024
BATCH = 2
SEQ = 16384
DEPTH = 4

MEM_LEN = 256
N_HEADS_MLA = 8
QK_NOPE_DIM = 64
QK_ROPE_DIM = 32
QK_HEAD_DIM = QK_NOPE_DIM + QK_ROPE_DIM
V_HEAD_DIM = 64
Q_LORA_RANK = 3 * D_MODEL // 8
KV_LORA_RANK = D_MODEL // 4
MLA_WIDTH = N_HEADS_MLA * V_HEAD_DIM
CONV_WIDTH = D_MODEL // 2
CONV_K = 3
N_HEADS_MEM = 4
MEM_HEAD_DIM = 128
MEM_WIDTH = N_HEADS_MEM * MEM_HEAD_DIM

N_BRANCH = 3
ROPE_BASE = 10000.0
Q_BLOCK = 128
EPS = 1e-6

IN_SIZES = (Q_LORA_RANK, KV_LORA_RANK, QK_ROPE_DIM,
            CONV_WIDTH, CONV_WIDTH, CONV_WIDTH,
            MEM_WIDTH,
            MLA_WIDTH, CONV_WIDTH, MEM_WIDTH,
            N_BRANCH * D_MODEL)
IN_WIDTH = sum(IN_SIZES)

kernel_name = 'hybrid_mla_shortconv_memxattn_encoder'


def rmsnorm(t, g):
    tf = t.astype(jnp.float32)
    tf = tf * lax.rsqrt(jnp.mean(tf * tf, axis=-1, keepdims=True) + EPS)
    return tf.astype(t.dtype) * g


def split_cols(t, sizes):
    return jnp.split(t, np.cumsum(sizes)[:-1].tolist(), axis=-1)


def rope_tables(positions, dtype):
    inv_freq = ROPE_BASE ** (-jnp.arange(0, QK_ROPE_DIM, 2, dtype=jnp.float32) / QK_ROPE_DIM)
    ang = positions.astype(jnp.float32)[..., None] * inv_freq
    return (jnp.cos(ang)[:, :, None, :].astype(dtype),
            jnp.sin(ang)[:, :, None, :].astype(dtype))


def rope_tail(t, cos, sin):
    t_nope, t1, t2 = split_cols(t, (QK_NOPE_DIM, QK_ROPE_DIM // 2, QK_ROPE_DIM // 2))
    return jnp.concatenate([t_nope, t1 * cos - t2 * sin, t2 * cos + t1 * sin], axis=-1)


def blocked_bidirectional_attention(q, k, v):
    B, S, H, Dh = q.shape
    nblk = S // Q_BLOCK
    qb = q.reshape(B, nblk, Q_BLOCK, H, Dh).transpose(1, 0, 2, 3, 4)
    scale = Dh ** -0.5

    def one_block(q_blk):
        s = jnp.einsum('bqhd,bkhd->bhqk', q_blk, k).astype(jnp.float32) * scale
        p = jax.nn.softmax(s, axis=-1).astype(v.dtype)
        return jnp.einsum('bhqk,bkhd->bqhd', p, v)

    out = lax.map(one_block, qb)
    return out.transpose(1, 0, 2, 3, 4).reshape(B, S, H * v.shape[-1])


def centred_short_conv(z, w, b):
    out = lax.conv_general_dilated(
        z, w[:, None, :], window_strides=(1,),
        padding=((CONV_K // 2, CONV_K // 2),),
        dimension_numbers=('NWC', 'WIO', 'NWC'),
        feature_group_count=z.shape[-1])
    return out + b


def hybrid_layer(x, mem, cos, sin, norm_g, w_in, b_gate, q_norm_g, w_uq, kv_norm_g, w_ukv,
                 q_head_g, k_head_g, conv_w, conv_b, mem_norm_g, w_mkv, mem_q_g, mem_k_g,
                 w_br_attn, w_br_conv, w_br_mem, w_out):
    B, S, _ = x.shape
    M = mem.shape[1]
    h = rmsnorm(x, norm_g)
    proj = h @ w_in
    (q_lat, kv_lat, k_pe, c_b, c_c, c_u, q_mem,
     g_attn, g_conv, g_mem, r) = split_cols(proj, IN_SIZES)

    q = (rmsnorm(q_lat, q_norm_g) @ w_uq).reshape(B, S, N_HEADS_MLA, QK_HEAD_DIM)
    kv = (rmsnorm(kv_lat, kv_norm_g) @ w_ukv).reshape(B, S, N_HEADS_MLA, QK_NOPE_DIM + V_HEAD_DIM)
    k_nope, v = split_cols(kv, (QK_NOPE_DIM, V_HEAD_DIM))
    k_rope = jnp.broadcast_to(k_pe[:, :, None, :], (B, S, N_HEADS_MLA, QK_ROPE_DIM))
    k = jnp.concatenate([k_nope, k_rope], axis=-1)
    q = rope_tail(rmsnorm(q, q_head_g), cos, sin)
    k = rope_tail(rmsnorm(k, k_head_g), cos, sin)
    o_attn = blocked_bidirectional_attention(q, k, v) * jax.nn.silu(g_attn)

    o_conv = c_b * centred_short_conv(c_c * c_u, conv_w, conv_b) * jax.nn.silu(g_conv)

    mkv = (rmsnorm(mem, mem_norm_g) @ w_mkv).reshape(B, M, N_HEADS_MEM, 2 * MEM_HEAD_DIM)
    m_k, m_v = split_cols(mkv, (MEM_HEAD_DIM, MEM_HEAD_DIM))
    mq = rmsnorm(q_mem.reshape(B, S, N_HEADS_MEM, MEM_HEAD_DIM), mem_q_g)
    m_k = rmsnorm(m_k, mem_k_g)
    s = jnp.einsum('bshd,bmhd->bhsm', mq, m_k).astype(jnp.float32) * (MEM_HEAD_DIM ** -0.5)
    p = jax.nn.softmax(s, axis=-1).astype(m_v.dtype)
    o_mem = jnp.einsum('bhsm,bmhd->bshd', p, m_v).reshape(B, S, MEM_WIDTH) * jax.nn.silu(g_mem)

    r_attn, r_conv, r_mem = split_cols(jax.nn.sigmoid(r + b_gate), (D_MODEL, D_MODEL, D_MODEL))
    y = r_attn * (o_attn @ w_br_attn) + r_conv * (o_conv @ w_br_conv) + r_mem * (o_mem @ w_br_mem)
    return x + y @ w_out


def setup_inputs(seed: int = 0) -> dict:
    key = jax.random.key(seed)
    ks = jax.random.split(key, 24)

    def nrm(k, shape, scale):
        return jax.random.normal(k, shape, jnp.float32) * scale

    def gain(k, shape):
        return 1.0 + 0.1 * jax.random.normal(k, shape, jnp.float32)

    x = nrm(ks[0], (BATCH, SEQ, D_MODEL), 1.0)
    mem = nrm(ks[1], (BATCH, MEM_LEN, D_MODEL), 1.0)
    offset = jax.random.randint(ks[2], (BATCH, 1), 0, 1024, dtype=jnp.int32)
    positions = jnp.arange(SEQ, dtype=jnp.int32)[None, :] + offset
    return {
        'x': x,
        'mem': mem,
        'positions': positions,
        'norm_g': gain(ks[3], (DEPTH, D_MODEL)),
        'w_in': nrm(ks[4], (DEPTH, D_MODEL, IN_WIDTH), D_MODEL ** -0.5),
        'b_gate': nrm(ks[5], (DEPTH, N_BRANCH * D_MODEL), 0.1),
        'q_norm_g': gain(ks[6], (DEPTH, Q_LORA_RANK)),
        'w_uq': nrm(ks[7], (DEPTH, Q_LORA_RANK, N_HEADS_MLA * QK_HEAD_DIM), Q_LORA_RANK ** -0.5),
        'kv_norm_g': gain(ks[8], (DEPTH, KV_LORA_RANK)),
        'w_ukv': nrm(ks[9], (DEPTH, KV_LORA_RANK, N_HEADS_MLA * (QK_NOPE_DIM + V_HEAD_DIM)), KV_LORA_RANK ** -0.5),
        'q_head_g': gain(ks[10], (DEPTH, QK_HEAD_DIM)),
        'k_head_g': gain(ks[11], (DEPTH, QK_HEAD_DIM)),
        'conv_w': nrm(ks[12], (DEPTH, CONV_K, CONV_WIDTH), CONV_K ** -0.5),
        'conv_b': nrm(ks[13], (DEPTH, CONV_WIDTH), 0.1),
        'mem_norm_g': gain(ks[14], (DEPTH, D_MODEL)),
        'w_mkv': nrm(ks[15], (DEPTH, D_MODEL, 2 * MEM_WIDTH), D_MODEL ** -0.5),
        'mem_q_g': gain(ks[16], (DEPTH, MEM_HEAD_DIM)),
        'mem_k_g': gain(ks[17], (DEPTH, MEM_HEAD_DIM)),
        'w_br_attn': nrm(ks[18], (DEPTH, MLA_WIDTH, D_MODEL), MLA_WIDTH ** -0.5),
        'w_br_conv': nrm(ks[19], (DEPTH, CONV_WIDTH, D_MODEL), CONV_WIDTH ** -0.5),
        'w_br_mem': nrm(ks[20], (DEPTH, MEM_WIDTH, D_MODEL), MEM_WIDTH ** -0.5),
        'w_out': nrm(ks[21], (DEPTH, D_MODEL, D_MODEL), D_MODEL ** -0.5),
    }


def reference(x, mem, positions, norm_g, w_in, b_gate, q_norm_g, w_uq, kv_norm_g, w_ukv,
              q_head_g, k_head_g, conv_w, conv_b, mem_norm_g, w_mkv, mem_q_g, mem_k_g,
              w_br_attn, w_br_conv, w_br_mem, w_out):
    cos, sin = rope_tables(positions, x.dtype)
    for i in range(DEPTH):
        x = hybrid_layer(x, mem, cos, sin, norm_g[i], w_in[i], b_gate[i], q_norm_g[i], w_uq[i],
                         kv_norm_g[i], w_ukv[i], q_head_g[i], k_head_g[i], conv_w[i], conv_b[i],
                         mem_norm_g[i], w_mkv[i], mem_q_g[i], mem_k_g[i],
                         w_br_attn[i], w_br_conv[i], w_br_mem[i], w_out[i])
    return x
```

```python
import functools
import math

import jax
import jax.numpy as jnp
from jax import lax
from jax.experimental import pallas as pl
from jax.experimental.pallas import tpu as pltpu

D_MODEL = 1024
N_HEADS = 8
NOPE = 64
ROPE = 32
QK_DIM = NOPE + ROPE
V_DIM = 64
Q_RANK = 3 * D_MODEL // 8
KV_RANK = D_MODEL // 4
MLA_W = N_HEADS * V_DIM
CONV_W = D_MODEL // 2
MEM_HEADS = 4
MEM_HD = 128
MEM_W = MEM_HEADS * MEM_HD
N_BRANCH = 3
ROPE_BASE = 10000.0
EPS = 1e-6
LOG2E = math.log2(math.e)

LANES = 128
SUBLANES = 8
HEAD_PAD = LANES
VMEM_LIMIT = 56 * 1024 * 1024

OFF_KVLAT = Q_RANK
OFF_KPE = Q_RANK + KV_RANK
OFF_REST = OFF_KPE + ROPE
W1_COLS = OFF_KPE + 2 * HEAD_PAD
R_CB, R_CC, R_CU, R_QM, R_GA, R_GC, R_GM, R_R = (0, 512, 1024, 1536, 2048, 2560, 3072, 3584)
REST_COLS = R_R + N_BRANCH * D_MODEL

TS_TAB = 512
TS_PRE = 512
TS_POST = 256
TQ = 256
TK = 1024
HALO = SUBLANES

F32 = jnp.float32
BF16 = jnp.bfloat16
NT_DIMS = (((1,), (1,)), ((), ()))


def _rms(t, axis=-1):
    return t * lax.rsqrt(jnp.mean(t * t, axis=axis, keepdims=True) + EPS)


def _sigmoid(t):
    return 1.0 / (1.0 + jnp.exp(-t))


def _silu(t):
    return t * _sigmoid(t)


def _tables_kernel(pos_ref, invf_ref, cosT_ref, sinT_ref, cosF_ref, sinF_ref):
    pos = pos_ref[0].astype(F32)
    ang = invf_ref[...] * pos
    c = jnp.cos(ang)
    s = jnp.sin(ang)
    cosT_ref[0] = c
    sinT_ref[0] = s
    t = pos.shape[1]
    z64 = jnp.zeros((NOPE, t), F32)
    z32 = jnp.zeros((HEAD_PAD - QK_DIM, t), F32)
    cosF_ref[0] = jnp.concatenate([z64, c, c, z32], axis=0).T
    sinF_ref[0] = jnp.concatenate([z64, -s, s, z32], axis=0).T


def _rope_tables(positions, invf):
    b, s = positions.shape
    half = ROPE // 2
    return pl.pallas_call(
        _tables_kernel,
        grid=(b, s // TS_TAB),
        in_specs=[pl.BlockSpec((1, 1, TS_TAB), lambda i, j: (i, 0, j)),
                  pl.BlockSpec((half, 1), lambda i, j: (0, 0))],
        out_specs=[pl.BlockSpec((1, half, TS_TAB), lambda i, j: (i, 0, j)),
                   pl.BlockSpec((1, half, TS_TAB), lambda i, j: (i, 0, j)),
                   pl.BlockSpec((1, TS_TAB, HEAD_PAD), lambda i, j: (i, j, 0)),
                   pl.BlockSpec((1, TS_TAB, HEAD_PAD), lambda i, j: (i, j, 0))],
        out_shape=[jax.ShapeDtypeStruct((b, half, s), F32),
                   jax.ShapeDtypeStruct((b, half, s), F32),
                   jax.ShapeDtypeStruct((b, s, HEAD_PAD), F32),
                   jax.ShapeDtypeStruct((b, s, HEAD_PAD), F32)],
        compiler_params=pltpu.CompilerParams(dimension_semantics=("parallel", "parallel")),
        name="rope_tables",
    )(positions.reshape(b, 1, s), invf)


def _mem_kernel(mem_ref, g_ref, w_ref, kg_ref, mk_ref, mv_ref):
    m = mem_ref[0]
    mn = (_rms(m) * g_ref[0]).astype(BF16)
    mkv = jnp.dot(mn, w_ref[0], preferred_element_type=F32)
    for h in range(MEM_HEADS):
        kh = mkv[:, h * 2 * MEM_HD: h * 2 * MEM_HD + MEM_HD]
        vh = mkv[:, h * 2 * MEM_HD + MEM_HD: (h + 1) * 2 * MEM_HD]
        mk_ref[0, 0, h] = (_rms(kh) * kg_ref[0]).astype(BF16)
        mv_ref[0, 0, h] = vh.astype(BF16)


def _mem_kv(mem, mem_norm_g, w_mkv_bf, mem_k_g):
    depth = w_mkv_bf.shape[0]
    b, m, d = mem.shape
    return pl.pallas_call(
        _mem_kernel,
        grid=(depth, b),
        in_specs=[pl.BlockSpec((1, m, d), lambda l, i: (i, 0, 0)),
                  pl.BlockSpec((1, 1, d), lambda l, i: (l, 0, 0)),
                  pl.BlockSpec((1, d, 2 * MEM_W), lambda l, i: (l, 0, 0)),
                  pl.BlockSpec((1, 1, MEM_HD), lambda l, i: (l, 0, 0))],
        out_specs=[pl.BlockSpec((1, 1, MEM_HEADS, m, MEM_HD), lambda l, i: (l, i, 0, 0, 0)),
                   pl.BlockSpec((1, 1, MEM_HEADS, m, MEM_HD), lambda l, i: (l, i, 0, 0, 0))],
        out_shape=[jax.ShapeDtypeStruct((depth, b, MEM_HEADS, m, MEM_HD), BF16),
                   jax.ShapeDtypeStruct((depth, b, MEM_HEADS, m, MEM_HD), BF16)],
        compiler_params=pltpu.CompilerParams(dimension_semantics=("parallel", "parallel")),
        name="mem_kv",
    )(mem, mem_norm_g.reshape(depth, 1, d), w_mkv_bf, mem_k_g.reshape(depth, 1, MEM_HD))


def _pre_kernel(layer_ref, x_ref, ng_ref, w1_ref, qng_ref, kvng_ref, wuq_ref, wuk_ref, wuv_ref,
                qg_ref, gn_ref, ga_ref, gb_ref, cosT_ref, sinT_ref, cosF_ref, sinF_ref,
                qT_ref, k_ref, vT_ref):
    del layer_ref
    x = x_ref[0]
    hb = (_rms(x) * ng_ref[0]).astype(BF16)
    a = jnp.dot(hb, w1_ref[0], preferred_element_type=F32)
    qn = (_rms(a[:, :Q_RANK]) * qng_ref[0]).astype(BF16)
    kvn = (_rms(a[:, OFF_KVLAT:OFF_KPE]) * kvng_ref[0]).astype(BF16)
    pe = a[:, OFF_KPE:OFF_KPE + HEAD_PAD]
    pe_sw = a[:, OFF_KPE + HEAD_PAD:W1_COLS]

    qT = lax.dot_general(wuq_ref[0], qn, NT_DIMS, preferred_element_type=F32)
    cT = cosT_ref[0]
    sT = sinT_ref[0]
    qg = qg_ref[0]
    half = ROPE // 2
    for h in range(N_HEADS):
        qh = qT[h * HEAD_PAD:(h + 1) * HEAD_PAD]
        ms = jnp.sum(qh * qh, axis=0, keepdims=True) * (1.0 / QK_DIM)
        qh = qh * lax.rsqrt(ms + EPS) * qg
        t1 = qh[NOPE:NOPE + half]
        t2 = qh[NOPE + half:QK_DIM]
        qh = jnp.concatenate([qh[:NOPE], t1 * cT - t2 * sT, t2 * cT + t1 * sT, qh[QK_DIM:]], axis=0)
        qT_ref[0, h] = qh.astype(BF16)

    knope = jnp.dot(kvn, wuk_ref[0], preferred_element_type=F32)
    u = pe * (ga_ref[0] * cosF_ref[0]) + pe_sw * (gb_ref[0] * sinF_ref[0])
    ss_pe = jnp.sum(pe * pe, axis=-1, keepdims=True)
    gn = gn_ref[0]
    for h in range(N_HEADS):
        kn = knope[:, h * HEAD_PAD:(h + 1) * HEAD_PAD]
        ms = (jnp.sum(kn * kn, axis=-1, keepdims=True) + ss_pe) * (1.0 / QK_DIM)
        k_ref[0, h] = ((kn * gn + u) * lax.rsqrt(ms + EPS)).astype(BF16)

    vT = lax.dot_general(wuv_ref[0], kvn, NT_DIMS, preferred_element_type=F32)
    vT_ref[0] = vT.reshape(N_HEADS, V_DIM, vT.shape[-1]).astype(BF16)


def _pre_call(layer, x, p, tabs):
    b, s, d = x.shape
    cosT, sinT, cosF, sinF = tabs
    half = ROPE // 2
    wspec = lambda shape: pl.BlockSpec((1,) + shape, lambda i, j, l: (l[0],) + (0,) * len(shape))
    grid_spec = pltpu.PrefetchScalarGridSpec(
        num_scalar_prefetch=1,
        grid=(b, s // TS_PRE),
        in_specs=[pl.BlockSpec((1, TS_PRE, d), lambda i, j, l: (i, j, 0)),
                  wspec((1, d)), wspec((d, W1_COLS)), wspec((1, Q_RANK)), wspec((1, KV_RANK)),
                  wspec((N_HEADS * HEAD_PAD, Q_RANK)), wspec((KV_RANK, N_HEADS * HEAD_PAD)),
                  wspec((N_HEADS * V_DIM, KV_RANK)),
                  wspec((HEAD_PAD, 1)), wspec((1, HEAD_PAD)), wspec((1, HEAD_PAD)), wspec((1, HEAD_PAD)),
                  pl.BlockSpec((1, half, TS_PRE), lambda i, j, l: (i, 0, j)),
                  pl.BlockSpec((1, half, TS_PRE), lambda i, j, l: (i, 0, j)),
                  pl.BlockSpec((1, TS_PRE, HEAD_PAD), lambda i, j, l: (i, j, 0)),
                  pl.BlockSpec((1, TS_PRE, HEAD_PAD), lambda i, j, l: (i, j, 0))],
        out_specs=[pl.BlockSpec((1, N_HEADS, HEAD_PAD, TS_PRE), lambda i, j, l: (i, 0, 0, j)),
                   pl.BlockSpec((1, N_HEADS, TS_PRE, HEAD_PAD), lambda i, j, l: (i, 0, j, 0)),
                   pl.BlockSpec((1, N_HEADS, V_DIM, TS_PRE), lambda i, j, l: (i, 0, 0, j))],
    )
    return pl.pallas_call(
        _pre_kernel,
        grid_spec=grid_spec,
        out_shape=[jax.ShapeDtypeStruct((b, N_HEADS, HEAD_PAD, s), BF16),
                   jax.ShapeDtypeStruct((b, N_HEADS, s, HEAD_PAD), BF16),
                   jax.ShapeDtypeStruct((b, N_HEADS, V_DIM, s), BF16)],
        compiler_params=pltpu.CompilerParams(dimension_semantics=("parallel", "parallel"),
                                             vmem_limit_bytes=VMEM_LIMIT),
        name="pre_attn",
    )(layer, x, p["norm_g"], p["w1"], p["q_norm_g"], p["kv_norm_g"], p["w_uqT"], p["w_uk"], p["w_uvT"],
      p["qg"], p["gn"], p["ga"], p["gb"], cosT, sinT, cosF, sinF)


def _attn_kernel(qT_ref, k_ref, vT_ref, oT_ref, s_scr):
    qT = qT_ref[0, 0]
    tq = qT.shape[1]
    n_groups = k_ref.shape[2] // TK

    def scores(g, slot):
        start = pl.multiple_of(g * TK, TK)
        s = jnp.dot(k_ref[0, 0, pl.ds(start, TK), :], qT, preferred_element_type=F32)
        s_scr[slot] = s
        return jnp.max(s, axis=0, keepdims=True)

    def accumulate(g, slot, m, l, acc, m_grp):
        start = pl.multiple_of(g * TK, TK)
        m_new = jnp.maximum(m, m_grp)
        alpha = jnp.exp2(m - m_new)
        p = jnp.exp2(s_scr[slot] - m_new)
        l = l * alpha + jnp.sum(p, axis=0, keepdims=True)
        pv = jnp.dot(vT_ref[0, 0, :, pl.ds(start, TK)], p.astype(BF16), preferred_element_type=F32)
        return m_new, l, acc * alpha + pv

    def pair(i, carry):
        m, l, acc, m_grp = carry
        g = 2 * i
        m_grp1 = scores(g + 1, 1)
        m, l, acc = accumulate(g, 0, m, l, acc, m_grp)
        m_grp2 = scores(g + 2, 0)
        m, l, acc = accumulate(g + 1, 1, m, l, acc, m_grp1)
        return m, l, acc, m_grp2

    init = (jnp.full((1, tq), -1e30, F32), jnp.zeros((1, tq), F32), jnp.zeros((V_DIM, tq), F32), scores(0, 0))
    m, l, acc, m_grp = lax.fori_loop(0, n_groups // 2 - 1, pair, init)
    g = n_groups - 2
    m_grp1 = scores(g + 1, 1)
    m, l, acc = accumulate(g, 0, m, l, acc, m_grp)
    m, l, acc = accumulate(g + 1, 1, m, l, acc, m_grp1)
    oT_ref[0, 0] = acc / l


def _attn_call(qT, k, vT):
    b, nh, _, s = qT.shape
    assert (s // TK) % 2 == 0
    return pl.pallas_call(
        _attn_kernel,
        grid=(b, nh, s // TQ),
        in_specs=[pl.BlockSpec((1, 1, HEAD_PAD, TQ), lambda i, h, j: (i, h, 0, j)),
                  pl.BlockSpec((1, 1, s, HEAD_PAD), lambda i, h, j: (i, h, 0, 0)),
                  pl.BlockSpec((1, 1, V_DIM, s), lambda i, h, j: (i, h, 0, 0))],
        out_specs=pl.BlockSpec((1, 1, V_DIM, TQ), lambda i, h, j: (i, h, 0, j)),
        out_shape=jax.ShapeDtypeStruct((b, nh, V_DIM, s), F32),
        scratch_shapes=[pltpu.VMEM((2, TK, TQ), F32)],
        compiler_params=pltpu.CompilerParams(dimension_semantics=("parallel", "parallel", "arbitrary"),
                                             vmem_limit_bytes=VMEM_LIMIT),
        name="mla_attn",
    )(qT, k, vT)


def _post_kernel(layer_ref, x_ref, xp_ref, xn_ref, oT_ref, mk_ref, mv_ref, ng_ref, wr_ref, bg_ref,
                 cw_ref, cb_ref, mqg_ref, wa_ref, wc_ref, wm_ref, wo_ref, out_ref):
    del layer_ref
    j = pl.program_id(1)
    nj = pl.num_programs(1)
    x = x_ref[0]
    t = x.shape[0]
    ng = ng_ref[0]
    hb = (_rms(x) * ng).astype(BF16)
    xh = jnp.concatenate([xp_ref[0, 0], xn_ref[0, 0]], axis=0)
    hh = (_rms(xh) * ng).astype(BF16)
    h_ext = jnp.concatenate([hb, hh], axis=0)

    def proj(lhs, lo, hi):
        return jnp.dot(lhs, wr_ref[0, :, lo:hi], preferred_element_type=F32)

    ccu = proj(h_ext, R_CC, R_QM)
    z_ext = ccu[:, :CONV_W] * ccu[:, CONV_W:]
    z = z_ext[:t]
    z_before = z_ext[t + HALO - 1:t + HALO] * (j > 0).astype(F32)
    z_after = z_ext[t + HALO:t + HALO + 1] * (j < nj - 1).astype(F32)
    row = lax.broadcasted_iota(jnp.int32, z.shape, 0)
    z_prev = jnp.where(row == 0, z_before, pltpu.roll(z, 1, 0))
    z_next = jnp.where(row == t - 1, z_after, pltpu.roll(z, t - 1, 0))
    cw = cw_ref[0]
    conv = z_prev * cw[0:1] + z * cw[1:2] + z_next * cw[2:3] + cb_ref[0]
    o_conv = proj(hb, R_CB, R_CC) * conv * _silu(proj(hb, R_GC, R_GM))
    y_conv = jnp.dot(o_conv.astype(BF16), wc_ref[0], preferred_element_type=F32)

    oT = oT_ref[0]
    o_attn = oT.reshape(MLA_W, t).T * _silu(proj(hb, R_GA, R_GC))
    y_attn = jnp.dot(o_attn.astype(BF16), wa_ref[0], preferred_element_type=F32)

    qm = proj(hb, R_QM, R_GA)
    mqg = mqg_ref[0]
    heads = []
    for h in range(MEM_HEADS):
        qh = (_rms(qm[:, h * MEM_HD:(h + 1) * MEM_HD]) * mqg).astype(BF16)
        s = lax.dot_general(qh, mk_ref[0, 0, h], NT_DIMS, preferred_element_type=F32)
        s = s * (MEM_HD ** -0.5)
        p = jnp.exp(s - jnp.max(s, axis=-1, keepdims=True))
        l = jnp.sum(p, axis=-1, keepdims=True)
        oh = jnp.dot(p.astype(BF16), mv_ref[0, 0, h], preferred_element_type=F32)
        heads.append(oh / l)
    o_mem = jnp.concatenate(heads, axis=-1) * _silu(proj(hb, R_GM, R_R))
    y_mem = jnp.dot(o_mem.astype(BF16), wm_ref[0], preferred_element_type=F32)

    bg = bg_ref[0]
    r_a = _sigmoid(proj(hb, R_R, R_R + D_MODEL) + bg[:, :D_MODEL])
    y = r_a * y_attn
    r_c = _sigmoid(proj(hb, R_R + D_MODEL, R_R + 2 * D_MODEL) + bg[:, D_MODEL:2 * D_MODEL])
    y = y + r_c * y_conv
    r_m = _sigmoid(proj(hb, R_R + 2 * D_MODEL, R_R + 3 * D_MODEL) + bg[:, 2 * D_MODEL:])
    y = y + r_m * y_mem
    out_ref[0] = x + jnp.dot(y.astype(BF16), wo_ref[0], preferred_element_type=F32)


def _post_call(layer, x, oT, mk, mv, p):
    b, s, d = x.shape
    m = mk.shape[3]
    nblk = TS_POST // HALO
    last = s // HALO - 1
    x4 = x.reshape(b, s // HALO, HALO, d)
    wspec = lambda shape: pl.BlockSpec((1,) + shape, lambda i, j, l: (l[0],) + (0,) * len(shape))
    grid_spec = pltpu.PrefetchScalarGridSpec(
        num_scalar_prefetch=1,
        grid=(b, s // TS_POST),
        in_specs=[pl.BlockSpec((1, TS_POST, d), lambda i, j, l: (i, j, 0)),
                  pl.BlockSpec((1, 1, HALO, d), lambda i, j, l: (i, jnp.maximum(j * nblk - 1, 0), 0, 0)),
                  pl.BlockSpec((1, 1, HALO, d), lambda i, j, l: (i, jnp.minimum((j + 1) * nblk, last), 0, 0)),
                  pl.BlockSpec((1, N_HEADS, V_DIM, TS_POST), lambda i, j, l: (i, 0, 0, j)),
                  pl.BlockSpec((1, 1, MEM_HEADS, m, MEM_HD), lambda i, j, l: (l[0], i, 0, 0, 0)),
                  pl.BlockSpec((1, 1, MEM_HEADS, m, MEM_HD), lambda i, j, l: (l[0], i, 0, 0, 0)),
                  wspec((1, d)), wspec((d, REST_COLS)), wspec((1, N_BRANCH * d)),
                  wspec((3, CONV_W)), wspec((1, CONV_W)), wspec((1, MEM_HD)),
                  wspec((MLA_W, d)), wspec((CONV_W, d)), wspec((MEM_W, d)), wspec((d, d))],
        out_specs=pl.BlockSpec((1, TS_POST, d), lambda i, j, l: (i, j, 0)),
    )
    return pl.pallas_call(
        _post_kernel,
        grid_spec=grid_spec,
        out_shape=jax.ShapeDtypeStruct((b, s, d), F32),
        compiler_params=pltpu.CompilerParams(dimension_semantics=("parallel", "parallel"),
                                             vmem_limit_bytes=VMEM_LIMIT),
        name="post_attn",
    )(layer, x, x4, x4, oT, mk, mv, p["norm_g"], p["w_rest"], p["b_gate"], p["conv_w"], p["conv_b"],
      p["mem_q_g"], p["w_br_attn"], p["w_br_conv"], p["w_br_mem"], p["w_out"])


def _prepare_params(norm_g, w_in, b_gate, q_norm_g, w_uq, kv_norm_g, w_ukv, q_head_g, k_head_g,
                    conv_w, conv_b, mem_q_g, w_br_attn, w_br_conv, w_br_mem, w_out):
    depth = w_in.shape[0]
    d = D_MODEL
    half = ROPE // 2
    z = lambda n: jnp.zeros((depth, d, n), F32)
    t1 = w_in[:, :, OFF_KPE:OFF_KPE + half]
    t2 = w_in[:, :, OFF_KPE + half:OFF_REST]
    pad = HEAD_PAD - QK_DIM
    w1 = jnp.concatenate([w_in[:, :, :OFF_KPE], z(NOPE), t1, t2, z(pad), z(NOPE), t2, t1, z(pad)], axis=-1)

    w_uq4 = w_uq.reshape(depth, Q_RANK, N_HEADS, QK_DIM)
    w_uq_pad = jnp.pad(w_uq4, ((0, 0), (0, 0), (0, 0), (0, pad))).reshape(depth, Q_RANK, N_HEADS * HEAD_PAD)
    w_ukv4 = w_ukv.reshape(depth, KV_RANK, N_HEADS, NOPE + V_DIM)
    w_uk = jnp.pad(w_ukv4[..., :NOPE], ((0, 0), (0, 0), (0, 0), (0, HEAD_PAD - NOPE)))
    w_uv = w_ukv4[..., NOPE:].reshape(depth, KV_RANK, N_HEADS * V_DIM)

    zg = lambda n: jnp.zeros((depth, n), F32)
    q_scale = (QK_DIM ** -0.5) * LOG2E
    qg = jnp.concatenate([q_head_g * q_scale, zg(pad)], axis=-1)[..., None]
    g_nope, g1, g2 = k_head_g[:, :NOPE], k_head_g[:, NOPE:NOPE + half], k_head_g[:, NOPE + half:]
    gn = jnp.concatenate([g_nope, zg(HEAD_PAD - NOPE)], axis=-1)[:, None, :]
    ga = jnp.concatenate([zg(NOPE), g1, g2, zg(pad)], axis=-1)[:, None, :]
    gb = jnp.concatenate([zg(NOPE), g2, g1, zg(pad)], axis=-1)[:, None, :]
    return dict(
        norm_g=norm_g[:, None, :],
        w1=w1.astype(BF16),
        q_norm_g=q_norm_g[:, None, :],
        kv_norm_g=kv_norm_g[:, None, :],
        w_uqT=jnp.swapaxes(w_uq_pad, 1, 2).astype(BF16),
        w_uk=w_uk.reshape(depth, KV_RANK, N_HEADS * HEAD_PAD).astype(BF16),
        w_uvT=jnp.swapaxes(w_uv, 1, 2).astype(BF16),
        qg=qg, gn=gn, ga=ga, gb=gb,
        w_rest=w_in[:, :, OFF_REST:].astype(BF16),
        b_gate=b_gate[:, None, :],
        conv_w=conv_w,
        conv_b=conv_b[:, None, :],
        mem_q_g=mem_q_g[:, None, :],
        w_br_attn=w_br_attn.astype(BF16),
        w_br_conv=w_br_conv.astype(BF16),
        w_br_mem=w_br_mem.astype(BF16),
        w_out=w_out.astype(BF16),
    )


def kernel(x, mem, positions, norm_g, w_in, b_gate, q_norm_g, w_uq, kv_norm_g, w_ukv, q_head_g, k_head_g,
           conv_w, conv_b, mem_norm_g, w_mkv, mem_q_g, mem_k_g, w_br_attn, w_br_conv, w_br_mem, w_out):
    depth = w_in.shape[0]
    assert x.shape[-1] == D_MODEL and w_in.shape[-1] == OFF_REST + REST_COLS
    assert x.shape[1] % max(TS_PRE, TS_POST, TQ, TK, TS_TAB) == 0
    p = _prepare_params(norm_g, w_in, b_gate, q_norm_g, w_uq, kv_norm_g, w_ukv, q_head_g, k_head_g,
                        conv_w, conv_b, mem_q_g, w_br_attn, w_br_conv, w_br_mem, w_out)
    invf = ROPE_BASE ** (-jnp.arange(0, ROPE, 2, dtype=F32) / ROPE)
    tabs = _rope_tables(positions, invf[:, None])
    mk, mv = _mem_kv(mem, mem_norm_g, w_mkv.astype(BF16), mem_k_g)

    def layer_step(i, xc):
        layer = jnp.full((1,), i, jnp.int32)
        qT, k, vT = _pre_call(layer, xc, p, tabs)
        oT = _attn_call(qT, k, vT)
        return _post_call(layer, xc, oT, mk, mv, p)

    return lax.fori_loop(0, depth, layer_step, x)
```

```python
import functools
import math

import jax
import jax.numpy as jnp
from jax import lax
from jax.experimental import pallas as pl
from jax.experimental.pallas import tpu as pltpu

D_MODEL = 1024
N_HEADS = 8
NOPE = 64
ROPE = 32
QK_DIM = NOPE + ROPE
V_DIM = 64
Q_RANK = 3 * D_MODEL // 8
KV_RANK = D_MODEL // 4
MLA_W = N_HEADS * V_DIM
CONV_W = D_MODEL // 2
MEM_HEADS = 4
MEM_HD = 128
MEM_W = MEM_HEADS * MEM_HD
N_BRANCH = 3
ROPE_BASE = 10000.0
EPS = 1e-6
LOG2E = math.log2(math.e)

LANES = 128
SUBLANES = 8
HEAD_PAD = LANES
VMEM_LIMIT = 56 * 1024 * 1024

OFF_KVLAT = Q_RANK
OFF_KPE = Q_RANK + KV_RANK
OFF_REST = OFF_KPE + ROPE
W1_COLS = OFF_KPE + 2 * HEAD_PAD
R_CB, R_CC, R_CU, R_QM, R_GA, R_GC, R_GM, R_R = (0, 512, 1024, 1536, 2048, 2560, 3072, 3584)
REST_COLS = R_R + N_BRANCH * D_MODEL

TS_TAB = 512
TS_PRE = 512
TS_POST = 256
TQ = 256
TK = 1024
HALO = SUBLANES
TQB = 512
TKB = 512
CHUNKS_B = 8
MAX_SCORE_BOUND = 48.0

F32 = jnp.float32
BF16 = jnp.bfloat16
NT_DIMS = (((1,), (1,)), ((), ()))


def _rms(t, axis=-1):
    return t * lax.rsqrt(jnp.mean(t * t, axis=axis, keepdims=True) + EPS)


def _sigmoid(t):
    return 1.0 / (1.0 + jnp.exp(-t))


def _silu(t):
    return t * _sigmoid(t)


def _tables_kernel(pos_ref, invf_ref, cosT_ref, sinT_ref, cosF_ref, sinF_ref):
    pos = pos_ref[0].astype(F32)
    ang = invf_ref[...] * pos
    c = jnp.cos(ang)
    s = jnp.sin(ang)
    cosT_ref[0] = c
    sinT_ref[0] = s
    t = pos.shape[1]
    z64 = jnp.zeros((NOPE, t), F32)
    z32 = jnp.zeros((HEAD_PAD - QK_DIM, t), F32)
    cosF_ref[0] = jnp.concatenate([z64, c, c, z32], axis=0).T
    sinF_ref[0] = jnp.concatenate([z64, -s, s, z32], axis=0).T


def _rope_tables(positions, invf):
    b, s = positions.shape
    half = ROPE // 2
    return pl.pallas_call(
        _tables_kernel,
        grid=(b, s // TS_TAB),
        in_specs=[pl.BlockSpec((1, 1, TS_TAB), lambda i, j: (i, 0, j)),
                  pl.BlockSpec((half, 1), lambda i, j: (0, 0))],
        out_specs=[pl.BlockSpec((1, half, TS_TAB), lambda i, j: (i, 0, j)),
                   pl.BlockSpec((1, half, TS_TAB), lambda i, j: (i, 0, j)),
                   pl.BlockSpec((1, TS_TAB, HEAD_PAD), lambda i, j: (i, j, 0)),
                   pl.BlockSpec((1, TS_TAB, HEAD_PAD), lambda i, j: (i, j, 0))],
        out_shape=[jax.ShapeDtypeStruct((b, half, s), F32),
                   jax.ShapeDtypeStruct((b, half, s), F32),
                   jax.ShapeDtypeStruct((b, s, HEAD_PAD), F32),
                   jax.ShapeDtypeStruct((b, s, HEAD_PAD), F32)],
        compiler_params=pltpu.CompilerParams(dimension_semantics=("parallel", "parallel")),
        name="rope_tables",
    )(positions.reshape(b, 1, s), invf)


def _mem_kernel(mem_ref, g_ref, w_ref, kg_ref, mk_ref, mv_ref):
    m = mem_ref[0]
    mn = (_rms(m) * g_ref[0]).astype(BF16)
    mkv = jnp.dot(mn, w_ref[0], preferred_element_type=F32)
    for h in range(MEM_HEADS):
        kh = mkv[:, h * 2 * MEM_HD: h * 2 * MEM_HD + MEM_HD]
        vh = mkv[:, h * 2 * MEM_HD + MEM_HD: (h + 1) * 2 * MEM_HD]
        mk_ref[0, 0, h] = (_rms(kh) * kg_ref[0]).astype(BF16)
        mv_ref[0, 0, h] = vh.astype(BF16)


def _mem_kv(mem, mem_norm_g, w_mkv_bf, mem_k_g):
    depth = w_mkv_bf.shape[0]
    b, m, d = mem.shape
    return pl.pallas_call(
        _mem_kernel,
        grid=(depth, b),
        in_specs=[pl.BlockSpec((1, m, d), lambda l, i: (i, 0, 0)),
                  pl.BlockSpec((1, 1, d), lambda l, i: (l, 0, 0)),
                  pl.BlockSpec((1, d, 2 * MEM_W), lambda l, i: (l, 0, 0)),
                  pl.BlockSpec((1, 1, MEM_HD), lambda l, i: (l, 0, 0))],
        out_specs=[pl.BlockSpec((1, 1, MEM_HEADS, m, MEM_HD), lambda l, i: (l, i, 0, 0, 0)),
                   pl.BlockSpec((1, 1, MEM_HEADS, m, MEM_HD), lambda l, i: (l, i, 0, 0, 0))],
        out_shape=[jax.ShapeDtypeStruct((depth, b, MEM_HEADS, m, MEM_HD), BF16),
                   jax.ShapeDtypeStruct((depth, b, MEM_HEADS, m, MEM_HD), BF16)],
        compiler_params=pltpu.CompilerParams(dimension_semantics=("parallel", "parallel")),
        name="mem_kv",
    )(mem, mem_norm_g.reshape(depth, 1, d), w_mkv_bf, mem_k_g.reshape(depth, 1, MEM_HD))


def _pre_kernel(layer_ref, x_ref, ng_ref, w1_ref, qng_ref, kvng_ref, wuq_ref, wuk_ref, wuv_ref,
                qg_ref, qoff_ref, gn_ref, ga_ref, gb_ref, cosT_ref, sinT_ref, cosF_ref, sinF_ref,
                qT_ref, k_ref, vT_ref):
    del layer_ref
    x = x_ref[0]
    hb = (_rms(x) * ng_ref[0]).astype(BF16)
    a = jnp.dot(hb, w1_ref[0], preferred_element_type=F32)
    qn = (_rms(a[:, :Q_RANK]) * qng_ref[0]).astype(BF16)
    kvn = (_rms(a[:, OFF_KVLAT:OFF_KPE]) * kvng_ref[0]).astype(BF16)
    pe = a[:, OFF_KPE:OFF_KPE + HEAD_PAD]
    pe_sw = a[:, OFF_KPE + HEAD_PAD:W1_COLS]

    qT = lax.dot_general(wuq_ref[0], qn, NT_DIMS, preferred_element_type=F32)
    cT = cosT_ref[0]
    sT = sinT_ref[0]
    qg = qg_ref[0]
    qoff = qoff_ref[0]
    half = ROPE // 2
    for h in range(N_HEADS):
        qh = qT[h * HEAD_PAD:(h + 1) * HEAD_PAD]
        ms = jnp.sum(qh * qh, axis=0, keepdims=True) * (1.0 / QK_DIM)
        qh = qh * lax.rsqrt(ms + EPS) * qg + qoff
        t1 = qh[NOPE:NOPE + half]
        t2 = qh[NOPE + half:QK_DIM]
        qh = jnp.concatenate([qh[:NOPE], t1 * cT - t2 * sT, t2 * cT + t1 * sT, qh[QK_DIM:]], axis=0)
        qT_ref[0, h] = qh.astype(BF16)

    knope = jnp.dot(kvn, wuk_ref[0], preferred_element_type=F32)
    u = pe * (ga_ref[0] * cosF_ref[0]) + pe_sw * (gb_ref[0] * sinF_ref[0])
    ss_pe = jnp.sum(pe * pe, axis=-1, keepdims=True)
    gn = gn_ref[0]
    one_lane = (lax.broadcasted_iota(jnp.int32, (1, HEAD_PAD), 1) == QK_DIM).astype(F32)
    for h in range(N_HEADS):
        kn = knope[:, h * HEAD_PAD:(h + 1) * HEAD_PAD]
        ms = (jnp.sum(kn * kn, axis=-1, keepdims=True) + ss_pe) * (1.0 / QK_DIM)
        k_ref[0, h] = ((kn * gn + u) * lax.rsqrt(ms + EPS) + one_lane).astype(BF16)

    vT = lax.dot_general(wuv_ref[0], kvn, NT_DIMS, preferred_element_type=F32)
    vT_ref[0] = vT.reshape(N_HEADS, V_DIM, vT.shape[-1]).astype(BF16)


def _pre_call(layer, x, p, tabs):
    b, s, d = x.shape
    cosT, sinT, cosF, sinF = tabs
    half = ROPE // 2
    wspec = lambda shape: pl.BlockSpec((1,) + shape, lambda i, j, l: (l[0],) + (0,) * len(shape))
    grid_spec = pltpu.PrefetchScalarGridSpec(
        num_scalar_prefetch=1,
        grid=(b, s // TS_PRE),
        in_specs=[pl.BlockSpec((1, TS_PRE, d), lambda i, j, l: (i, j, 0)),
                  wspec((1, d)), wspec((d, W1_COLS)), wspec((1, Q_RANK)), wspec((1, KV_RANK)),
                  wspec((N_HEADS * HEAD_PAD, Q_RANK)), wspec((KV_RANK, N_HEADS * HEAD_PAD)),
                  wspec((N_HEADS * V_DIM, KV_RANK)),
                  wspec((HEAD_PAD, 1)), wspec((HEAD_PAD, 1)),
                  wspec((1, HEAD_PAD)), wspec((1, HEAD_PAD)), wspec((1, HEAD_PAD)),
                  pl.BlockSpec((1, half, TS_PRE), lambda i, j, l: (i, 0, j)),
                  pl.BlockSpec((1, half, TS_PRE), lambda i, j, l: (i, 0, j)),
                  pl.BlockSpec((1, TS_PRE, HEAD_PAD), lambda i, j, l: (i, j, 0)),
                  pl.BlockSpec((1, TS_PRE, HEAD_PAD), lambda i, j, l: (i, j, 0))],
        out_specs=[pl.BlockSpec((1, N_HEADS, HEAD_PAD, TS_PRE), lambda i, j, l: (i, 0, 0, j)),
                   pl.BlockSpec((1, N_HEADS, TS_PRE, HEAD_PAD), lambda i, j, l: (i, 0, j, 0)),
                   pl.BlockSpec((1, N_HEADS, V_DIM, TS_PRE), lambda i, j, l: (i, 0, 0, j))],
    )
    return pl.pallas_call(
        _pre_kernel,
        grid_spec=grid_spec,
        out_shape=[jax.ShapeDtypeStruct((b, N_HEADS, HEAD_PAD, s), BF16),
                   jax.ShapeDtypeStruct((b, N_HEADS, s, HEAD_PAD), BF16),
                   jax.ShapeDtypeStruct((b, N_HEADS, V_DIM, s), BF16)],
        compiler_params=pltpu.CompilerParams(dimension_semantics=("parallel", "parallel"),
                                             vmem_limit_bytes=VMEM_LIMIT),
        name="pre_attn",
    )(layer, x, p["norm_g"], p["w1"], p["q_norm_g"], p["kv_norm_g"], p["w_uqT"], p["w_uk"], p["w_uvT"],
      p["qg"], p["qoff"], p["gn"], p["ga"], p["gb"], cosT, sinT, cosF, sinF)


def _attn_kernel(qT_ref, k_ref, vT_ref, oT_ref, s_scr):
    qT = qT_ref[0, 0]
    tq = qT.shape[1]
    n_groups = k_ref.shape[2] // TK

    def scores(g, slot):
        start = pl.multiple_of(g * TK, TK)
        s = jnp.dot(k_ref[0, 0, pl.ds(start, TK), :], qT, preferred_element_type=F32)
        s_scr[slot] = s
        return jnp.max(s, axis=0, keepdims=True)

    def accumulate(g, slot, m, l, acc, m_grp):
        start = pl.multiple_of(g * TK, TK)
        m_new = jnp.maximum(m, m_grp)
        alpha = jnp.exp2(m - m_new)
        p = jnp.exp2(s_scr[slot] - m_new)
        l = l * alpha + jnp.sum(p, axis=0, keepdims=True)
        pv = jnp.dot(vT_ref[0, 0, :, pl.ds(start, TK)], p.astype(BF16), preferred_element_type=F32)
        return m_new, l, acc * alpha + pv

    def pair(i, carry):
        m, l, acc, m_grp = carry
        g = 2 * i
        m_grp1 = scores(g + 1, 1)
        m, l, acc = accumulate(g, 0, m, l, acc, m_grp)
        m_grp2 = scores(g + 2, 0)
        m, l, acc = accumulate(g + 1, 1, m, l, acc, m_grp1)
        return m, l, acc, m_grp2

    init = (jnp.full((1, tq), -1e30, F32), jnp.zeros((1, tq), F32), jnp.zeros((V_DIM, tq), F32), scores(0, 0))
    m, l, acc, m_grp = lax.fori_loop(0, n_groups // 2 - 1, pair, init)
    g = n_groups - 2
    m_grp1 = scores(g + 1, 1)
    m, l, acc = accumulate(g, 0, m, l, acc, m_grp)
    m, l, acc = accumulate(g + 1, 1, m, l, acc, m_grp1)
    oT_ref[0, 0] = acc / l


def _attn_bounded_kernel(qT_ref, k_ref, vT_ref, oT_ref, s_scr):
    qT = qT_ref[0, 0]
    tq = qT.shape[1]
    n_chunks = k_ref.shape[2] // TKB

    def scores(c, slot):
        start = pl.multiple_of(c * TKB, TKB)
        s_scr[slot] = jnp.dot(k_ref[0, 0, pl.ds(start, TKB), :], qT, preferred_element_type=F32)

    def consume(c, slot, l8, acc):
        start = pl.multiple_of(c * TKB, TKB)
        p = jnp.exp2(s_scr[slot])
        l8 = l8 + jnp.sum(p.reshape(TKB // SUBLANES, SUBLANES, tq), axis=0)
        acc = acc + jnp.dot(vT_ref[0, 0, :, pl.ds(start, TKB)], p.astype(BF16), preferred_element_type=F32)
        return l8, acc

    def trip(i, carry, last=False):
        l8, acc = carry
        for u in range(CHUNKS_B):
            c = i * CHUNKS_B + u
            if not (last and u == CHUNKS_B - 1):
                scores(c + 1, (u + 1) % 2)
            l8, acc = consume(c, u % 2, l8, acc)
        return l8, acc

    scores(0, 0)
    init = (jnp.zeros((SUBLANES, tq), F32), jnp.zeros((V_DIM, tq), F32))
    n_trips = n_chunks // CHUNKS_B
    carry = lax.fori_loop(0, n_trips - 1, trip, init)
    l8, acc = trip(n_trips - 1, carry, last=True)
    oT_ref[0, 0] = acc / jnp.sum(l8, axis=0, keepdims=True)


def _attn_bounded_call(qT, k, vT):
    b, nh, _, s = qT.shape
    assert CHUNKS_B % 2 == 0 and s % (CHUNKS_B * TKB) == 0
    return pl.pallas_call(
        _attn_bounded_kernel,
        grid=(b, nh, s // TQB),
        in_specs=[pl.BlockSpec((1, 1, HEAD_PAD, TQB), lambda i, h, j: (i, h, 0, j)),
                  pl.BlockSpec((1, 1, s, HEAD_PAD), lambda i, h, j: (i, h, 0, 0)),
                  pl.BlockSpec((1, 1, V_DIM, s), lambda i, h, j: (i, h, 0, 0))],
        out_specs=pl.BlockSpec((1, 1, V_DIM, TQB), lambda i, h, j: (i, h, 0, j)),
        out_shape=jax.ShapeDtypeStruct((b, nh, V_DIM, s), F32),
        scratch_shapes=[pltpu.VMEM((2, TKB, TQB), F32)],
        compiler_params=pltpu.CompilerParams(dimension_semantics=("parallel", "parallel", "arbitrary"),
                                             vmem_limit_bytes=VMEM_LIMIT),
        name="mla_attn_bounded",
    )(qT, k, vT)


def _attn_call(qT, k, vT):
    b, nh, _, s = qT.shape
    assert (s // TK) % 2 == 0
    return pl.pallas_call(
        _attn_kernel,
        grid=(b, nh, s // TQ),
        in_specs=[pl.BlockSpec((1, 1, HEAD_PAD, TQ), lambda i, h, j: (i, h, 0, j)),
                  pl.BlockSpec((1, 1, s, HEAD_PAD), lambda i, h, j: (i, h, 0, 0)),
                  pl.BlockSpec((1, 1, V_DIM, s), lambda i, h, j: (i, h, 0, 0))],
        out_specs=pl.BlockSpec((1, 1, V_DIM, TQ), lambda i, h, j: (i, h, 0, j)),
        out_shape=jax.ShapeDtypeStruct((b, nh, V_DIM, s), F32),
        scratch_shapes=[pltpu.VMEM((2, TK, TQ), F32)],
        compiler_params=pltpu.CompilerParams(dimension_semantics=("parallel", "parallel", "arbitrary"),
                                             vmem_limit_bytes=VMEM_LIMIT),
        name="mla_attn",
    )(qT, k, vT)


def _post_kernel(layer_ref, x_ref, xp_ref, xn_ref, oT_ref, mk_ref, mv_ref, ng_ref, wr_ref, bg_ref,
                 cw_ref, cb_ref, mqg_ref, wa_ref, wc_ref, wm_ref, wo_ref, out_ref):
    del layer_ref
    j = pl.program_id(1)
    nj = pl.num_programs(1)
    x = x_ref[0]
    t = x.shape[0]
    ng = ng_ref[0]
    hb = (_rms(x) * ng).astype(BF16)
    xh = jnp.concatenate([xp_ref[0, 0], xn_ref[0, 0]], axis=0)
    hh = (_rms(xh) * ng).astype(BF16)
    h_ext = jnp.concatenate([hb, hh], axis=0)

    def proj(lhs, lo, hi):
        return jnp.dot(lhs, wr_ref[0, :, lo:hi], preferred_element_type=F32)

    ccu = proj(h_ext, R_CC, R_QM)
    z_ext = ccu[:, :CONV_W] * ccu[:, CONV_W:]
    z = z_ext[:t]
    z_before = z_ext[t + HALO - 1:t + HALO] * (j > 0).astype(F32)
    z_after = z_ext[t + HALO:t + HALO + 1] * (j < nj - 1).astype(F32)
    row = lax.broadcasted_iota(jnp.int32, z.shape, 0)
    z_prev = jnp.where(row == 0, z_before, pltpu.roll(z, 1, 0))
    z_next = jnp.where(row == t - 1, z_after, pltpu.roll(z, t - 1, 0))
    cw = cw_ref[0]
    conv = z_prev * cw[0:1] + z * cw[1:2] + z_next * cw[2:3] + cb_ref[0]
    o_conv = proj(hb, R_CB, R_CC) * conv * _silu(proj(hb, R_GC, R_GM))
    y_conv = jnp.dot(o_conv.astype(BF16), wc_ref[0], preferred_element_type=F32)

    oT = oT_ref[0]
    o_attn = oT.reshape(MLA_W, t).T * _silu(proj(hb, R_GA, R_GC))
    y_attn = jnp.dot(o_attn.astype(BF16), wa_ref[0], preferred_element_type=F32)

    qm = proj(hb, R_QM, R_GA)
    mqg = mqg_ref[0]
    heads = []
    for h in range(MEM_HEADS):
        qh = (_rms(qm[:, h * MEM_HD:(h + 1) * MEM_HD]) * mqg).astype(BF16)
        s = lax.dot_general(qh, mk_ref[0, 0, h], NT_DIMS, preferred_element_type=F32)
        s = s * (MEM_HD ** -0.5)
        p = jnp.exp(s - jnp.max(s, axis=-1, keepdims=True))
        l = jnp.sum(p, axis=-1, keepdims=True)
        oh = jnp.dot(p.astype(BF16), mv_ref[0, 0, h], preferred_element_type=F32)
        heads.append(oh / l)
    o_mem = jnp.concatenate(heads, axis=-1) * _silu(proj(hb, R_GM, R_R))
    y_mem = jnp.dot(o_mem.astype(BF16), wm_ref[0], preferred_element_type=F32)

    bg = bg_ref[0]
    r_a = _sigmoid(proj(hb, R_R, R_R + D_MODEL) + bg[:, :D_MODEL])
    y = r_a * y_attn
    r_c = _sigmoid(proj(hb, R_R + D_MODEL, R_R + 2 * D_MODEL) + bg[:, D_MODEL:2 * D_MODEL])
    y = y + r_c * y_conv
    r_m = _sigmoid(proj(hb, R_R + 2 * D_MODEL, R_R + 3 * D_MODEL) + bg[:, 2 * D_MODEL:])
    y = y + r_m * y_mem
    out_ref[0] = x + jnp.dot(y.astype(BF16), wo_ref[0], preferred_element_type=F32)


def _post_call(layer, x, oT, mk, mv, p):
    b, s, d = x.shape
    m = mk.shape[3]
    nblk = TS_POST // HALO
    last = s // HALO - 1
    x4 = x.reshape(b, s // HALO, HALO, d)
    wspec = lambda shape: pl.BlockSpec((1,) + shape, lambda i, j, l: (l[0],) + (0,) * len(shape))
    grid_spec = pltpu.PrefetchScalarGridSpec(
        num_scalar_prefetch=1,
        grid=(b, s // TS_POST),
        in_specs=[pl.BlockSpec((1, TS_POST, d), lambda i, j, l: (i, j, 0)),
                  pl.BlockSpec((1, 1, HALO, d), lambda i, j, l: (i, jnp.maximum(j * nblk - 1, 0), 0, 0)),
                  pl.BlockSpec((1, 1, HALO, d), lambda i, j, l: (i, jnp.minimum((j + 1) * nblk, last), 0, 0)),
                  pl.BlockSpec((1, N_HEADS, V_DIM, TS_POST), lambda i, j, l: (i, 0, 0, j)),
                  pl.BlockSpec((1, 1, MEM_HEADS, m, MEM_HD), lambda i, j, l: (l[0], i, 0, 0, 0)),
                  pl.BlockSpec((1, 1, MEM_HEADS, m, MEM_HD), lambda i, j, l: (l[0], i, 0, 0, 0)),
                  wspec((1, d)), wspec((d, REST_COLS)), wspec((1, N_BRANCH * d)),
                  wspec((3, CONV_W)), wspec((1, CONV_W)), wspec((1, MEM_HD)),
                  wspec((MLA_W, d)), wspec((CONV_W, d)), wspec((MEM_W, d)), wspec((d, d))],
        out_specs=pl.BlockSpec((1, TS_POST, d), lambda i, j, l: (i, j, 0)),
    )
    return pl.pallas_call(
        _post_kernel,
        grid_spec=grid_spec,
        out_shape=jax.ShapeDtypeStruct((b, s, d), F32),
        compiler_params=pltpu.CompilerParams(dimension_semantics=("parallel", "parallel"),
                                             vmem_limit_bytes=VMEM_LIMIT),
        name="post_attn",
    )(layer, x, x4, x4, oT, mk, mv, p["norm_g"], p["w_rest"], p["b_gate"], p["conv_w"], p["conv_b"],
      p["mem_q_g"], p["w_br_attn"], p["w_br_conv"], p["w_br_mem"], p["w_out"])


def _prepare_params(norm_g, w_in, b_gate, q_norm_g, w_uq, kv_norm_g, w_ukv, q_head_g, k_head_g,
                    conv_w, conv_b, mem_q_g, w_br_attn, w_br_conv, w_br_mem, w_out):
    depth = w_in.shape[0]
    d = D_MODEL
    half = ROPE // 2
    z = lambda n: jnp.zeros((depth, d, n), F32)
    t1 = w_in[:, :, OFF_KPE:OFF_KPE + half]
    t2 = w_in[:, :, OFF_KPE + half:OFF_REST]
    pad = HEAD_PAD - QK_DIM
    w1 = jnp.concatenate([w_in[:, :, :OFF_KPE], z(NOPE), t1, t2, z(pad), z(NOPE), t2, t1, z(pad)], axis=-1)

    w_uq4 = w_uq.reshape(depth, Q_RANK, N_HEADS, QK_DIM)
    w_uq_pad = jnp.pad(w_uq4, ((0, 0), (0, 0), (0, 0), (0, pad))).reshape(depth, Q_RANK, N_HEADS * HEAD_PAD)
    w_ukv4 = w_ukv.reshape(depth, KV_RANK, N_HEADS, NOPE + V_DIM)
    w_uk = jnp.pad(w_ukv4[..., :NOPE], ((0, 0), (0, 0), (0, 0), (0, HEAD_PAD - NOPE)))
    w_uv = w_ukv4[..., NOPE:].reshape(depth, KV_RANK, N_HEADS * V_DIM)

    zg = lambda n: jnp.zeros((depth, n), F32)
    q_scale = (QK_DIM ** -0.5) * LOG2E
    qg = jnp.concatenate([q_head_g * q_scale, zg(pad)], axis=-1)[..., None]
    score_bound = (QK_DIM ** 0.5) * LOG2E * jnp.max(jnp.abs(q_head_g), axis=-1) * jnp.max(jnp.abs(k_head_g), axis=-1)
    bounded = score_bound <= MAX_SCORE_BOUND
    off = jnp.where(bounded, -score_bound, 0.0)
    qoff = jnp.concatenate([zg(QK_DIM), jnp.broadcast_to(off[:, None], (depth, pad))], axis=-1)[..., None]
    g_nope, g1, g2 = k_head_g[:, :NOPE], k_head_g[:, NOPE:NOPE + half], k_head_g[:, NOPE + half:]
    gn = jnp.concatenate([g_nope, zg(HEAD_PAD - NOPE)], axis=-1)[:, None, :]
    ga = jnp.concatenate([zg(NOPE), g1, g2, zg(pad)], axis=-1)[:, None, :]
    gb = jnp.concatenate([zg(NOPE), g2, g1, zg(pad)], axis=-1)[:, None, :]
    return dict(
        norm_g=norm_g[:, None, :],
        w1=w1.astype(BF16),
        q_norm_g=q_norm_g[:, None, :],
        kv_norm_g=kv_norm_g[:, None, :],
        w_uqT=jnp.swapaxes(w_uq_pad, 1, 2).astype(BF16),
        w_uk=w_uk.reshape(depth, KV_RANK, N_HEADS * HEAD_PAD).astype(BF16),
        w_uvT=jnp.swapaxes(w_uv, 1, 2).astype(BF16),
        qg=qg, qoff=qoff, bounded=bounded, gn=gn, ga=ga, gb=gb,
        w_rest=w_in[:, :, OFF_REST:].astype(BF16),
        b_gate=b_gate[:, None, :],
        conv_w=conv_w,
        conv_b=conv_b[:, None, :],
        mem_q_g=mem_q_g[:, None, :],
        w_br_attn=w_br_attn.astype(BF16),
        w_br_conv=w_br_conv.astype(BF16),
        w_br_mem=w_br_mem.astype(BF16),
        w_out=w_out.astype(BF16),
    )


def kernel(x, mem, positions, norm_g, w_in, b_gate, q_norm_g, w_uq, kv_norm_g, w_ukv, q_head_g, k_head_g,
           conv_w, conv_b, mem_norm_g, w_mkv, mem_q_g, mem_k_g, w_br_attn, w_br_conv, w_br_mem, w_out):
    depth = w_in.shape[0]
    assert x.shape[-1] == D_MODEL and w_in.shape[-1] == OFF_REST + REST_COLS
    assert x.shape[1] % max(TS_PRE, TS_POST, TQ, TK, TS_TAB) == 0
    p = _prepare_params(norm_g, w_in, b_gate, q_norm_g, w_uq, kv_norm_g, w_ukv, q_head_g, k_head_g,
                        conv_w, conv_b, mem_q_g, w_br_attn, w_br_conv, w_br_mem, w_out)
    invf = ROPE_BASE ** (-jnp.arange(0, ROPE, 2, dtype=F32) / ROPE)
    tabs = _rope_tables(positions, invf[:, None])
    mk, mv = _mem_kv(mem, mem_norm_g, w_mkv.astype(BF16), mem_k_g)

    def layer_step(i, xc):
        layer = jnp.full((1,), i, jnp.int32)
        qT, k, vT = _pre_call(layer, xc, p, tabs)
        oT = lax.cond(p["bounded"][i], _attn_bounded_call, _attn_call, qT, k, vT)
        return _post_call(layer, xc, oT, mk, mv, p)

    return lax.fori_loop(0, depth, layer_step, x)
```

```python
import functools
import math

import jax
import jax.numpy as jnp
from jax import lax
from jax.experimental import pallas as pl
from jax.experimental.pallas import tpu as pltpu

D_MODEL = 1024
N_HEADS = 8
NOPE = 64
ROPE = 32
QK_DIM = NOPE + ROPE
V_DIM = 64
Q_RANK = 3 * D_MODEL // 8
KV_RANK = D_MODEL // 4
MLA_W = N_HEADS * V_DIM
CONV_W = D_MODEL // 2
MEM_HEADS = 4
MEM_HD = 128
MEM_W = MEM_HEADS * MEM_HD
N_BRANCH = 3
ROPE_BASE = 10000.0
EPS = 1e-6
LOG2E = math.log2(math.e)

LANES = 128
SUBLANES = 8
HEAD_PAD = LANES
VMEM_LIMIT = 56 * 1024 * 1024

OFF_KVLAT = Q_RANK
OFF_KPE = Q_RANK + KV_RANK
OFF_REST = OFF_KPE + ROPE
W1_COLS = OFF_KPE + 2 * HEAD_PAD
R_CB, R_CC, R_CU, R_QM, R_GA, R_GC, R_GM, R_R = (0, 512, 1024, 1536, 2048, 2560, 3072, 3584)
REST_COLS = R_R + N_BRANCH * D_MODEL

TS_TAB = 512
TS_PRE = 512
TS_POST = 256
TQ = 256
TK = 1024
HALO = SUBLANES
TQB = 512
TKB = 512
CHUNKS_B = 32
MAX_SCORE_BOUND = 48.0

F32 = jnp.float32
BF16 = jnp.bfloat16
NT_DIMS = (((1,), (1,)), ((), ()))


def _rms(t, axis=-1):
    return t * lax.rsqrt(jnp.mean(t * t, axis=axis, keepdims=True) + EPS)


def _sigmoid(t):
    return 1.0 / (1.0 + jnp.exp(-t))


def _silu(t):
    return t * _sigmoid(t)


def _tables_kernel(pos_ref, invf_ref, cosT_ref, sinT_ref, cosF_ref, sinF_ref):
    pos = pos_ref[0].astype(F32)
    ang = invf_ref[...] * pos
    c = jnp.cos(ang)
    s = jnp.sin(ang)
    cosT_ref[0] = c
    sinT_ref[0] = s
    t = pos.shape[1]
    z64 = jnp.zeros((NOPE, t), F32)
    z32 = jnp.zeros((HEAD_PAD - QK_DIM, t), F32)
    cosF_ref[0] = jnp.concatenate([z64, c, c, z32], axis=0).T
    sinF_ref[0] = jnp.concatenate([z64, -s, s, z32], axis=0).T


def _rope_tables(positions, invf):
    b, s = positions.shape
    half = ROPE // 2
    return pl.pallas_call(
        _tables_kernel,
        grid=(b, s // TS_TAB),
        in_specs=[pl.BlockSpec((1, 1, TS_TAB), lambda i, j: (i, 0, j)),
                  pl.BlockSpec((half, 1), lambda i, j: (0, 0))],
        out_specs=[pl.BlockSpec((1, half, TS_TAB), lambda i, j: (i, 0, j)),
                   pl.BlockSpec((1, half, TS_TAB), lambda i, j: (i, 0, j)),
                   pl.BlockSpec((1, TS_TAB, HEAD_PAD), lambda i, j: (i, j, 0)),
                   pl.BlockSpec((1, TS_TAB, HEAD_PAD), lambda i, j: (i, j, 0))],
        out_shape=[jax.ShapeDtypeStruct((b, half, s), F32),
                   jax.ShapeDtypeStruct((b, half, s), F32),
                   jax.ShapeDtypeStruct((b, s, HEAD_PAD), F32),
                   jax.ShapeDtypeStruct((b, s, HEAD_PAD), F32)],
        compiler_params=pltpu.CompilerParams(dimension_semantics=("parallel", "parallel")),
        name="rope_tables",
    )(positions.reshape(b, 1, s), invf)


def _mem_kernel(mem_ref, g_ref, w_ref, kg_ref, mk_ref, mv_ref):
    m = mem_ref[0]
    mn = (_rms(m) * g_ref[0]).astype(BF16)
    mkv = jnp.dot(mn, w_ref[0], preferred_element_type=F32)
    for h in range(MEM_HEADS):
        kh = mkv[:, h * 2 * MEM_HD: h * 2 * MEM_HD + MEM_HD]
        vh = mkv[:, h * 2 * MEM_HD + MEM_HD: (h + 1) * 2 * MEM_HD]
        mk_ref[0, 0, h] = (_rms(kh) * kg_ref[0]).astype(BF16)
        mv_ref[0, 0, h] = vh.astype(BF16)


def _mem_kv(mem, mem_norm_g, w_mkv_bf, mem_k_g):
    depth = w_mkv_bf.shape[0]
    b, m, d = mem.shape
    return pl.pallas_call(
        _mem_kernel,
        grid=(depth, b),
        in_specs=[pl.BlockSpec((1, m, d), lambda l, i: (i, 0, 0)),
                  pl.BlockSpec((1, 1, d), lambda l, i: (l, 0, 0)),
                  pl.BlockSpec((1, d, 2 * MEM_W), lambda l, i: (l, 0, 0)),
                  pl.BlockSpec((1, 1, MEM_HD), lambda l, i: (l, 0, 0))],
        out_specs=[pl.BlockSpec((1, 1, MEM_HEADS, m, MEM_HD), lambda l, i: (l, i, 0, 0, 0)),
                   pl.BlockSpec((1, 1, MEM_HEADS, m, MEM_HD), lambda l, i: (l, i, 0, 0, 0))],
        out_shape=[jax.ShapeDtypeStruct((depth, b, MEM_HEADS, m, MEM_HD), BF16),
                   jax.ShapeDtypeStruct((depth, b, MEM_HEADS, m, MEM_HD), BF16)],
        compiler_params=pltpu.CompilerParams(dimension_semantics=("parallel", "parallel")),
        name="mem_kv",
    )(mem, mem_norm_g.reshape(depth, 1, d), w_mkv_bf, mem_k_g.reshape(depth, 1, MEM_HD))


def _pre_kernel(layer_ref, x_ref, ng_ref, w1_ref, qng_ref, kvng_ref, wuq_ref, wuk_ref, wuv_ref,
                qg_ref, qoff_ref, gn_ref, ga_ref, gb_ref, cosT_ref, sinT_ref, cosF_ref, sinF_ref,
                qT_ref, k_ref, vT_ref):
    del layer_ref
    x = x_ref[0]
    hb = (_rms(x) * ng_ref[0]).astype(BF16)
    a = jnp.dot(hb, w1_ref[0], preferred_element_type=F32)
    qn = (_rms(a[:, :Q_RANK]) * qng_ref[0]).astype(BF16)
    kvn = (_rms(a[:, OFF_KVLAT:OFF_KPE]) * kvng_ref[0]).astype(BF16)
    pe = a[:, OFF_KPE:OFF_KPE + HEAD_PAD]
    pe_sw = a[:, OFF_KPE + HEAD_PAD:W1_COLS]

    qT = lax.dot_general(wuq_ref[0], qn, NT_DIMS, preferred_element_type=F32)
    cT = cosT_ref[0]
    sT = sinT_ref[0]
    qg = qg_ref[0]
    qoff = qoff_ref[0]
    half = ROPE // 2
    for h in range(N_HEADS):
        qh = qT[h * HEAD_PAD:(h + 1) * HEAD_PAD]
        ms = jnp.sum(qh * qh, axis=0, keepdims=True) * (1.0 / QK_DIM)
        qh = qh * lax.rsqrt(ms + EPS) * qg + qoff
        t1 = qh[NOPE:NOPE + half]
        t2 = qh[NOPE + half:QK_DIM]
        qh = jnp.concatenate([qh[:NOPE], t1 * cT - t2 * sT, t2 * cT + t1 * sT, qh[QK_DIM:]], axis=0)
        qT_ref[0, h] = qh.astype(BF16)

    knope = jnp.dot(kvn, wuk_ref[0], preferred_element_type=F32)
    u = pe * (ga_ref[0] * cosF_ref[0]) + pe_sw * (gb_ref[0] * sinF_ref[0])
    ss_pe = jnp.sum(pe * pe, axis=-1, keepdims=True)
    gn = gn_ref[0]
    one_lane = (lax.broadcasted_iota(jnp.int32, (1, HEAD_PAD), 1) == QK_DIM).astype(F32)
    for h in range(N_HEADS):
        kn = knope[:, h * HEAD_PAD:(h + 1) * HEAD_PAD]
        ms = (jnp.sum(kn * kn, axis=-1, keepdims=True) + ss_pe) * (1.0 / QK_DIM)
        k_ref[0, h] = ((kn * gn + u) * lax.rsqrt(ms + EPS) + one_lane).astype(BF16)

    vT = lax.dot_general(wuv_ref[0], kvn, NT_DIMS, preferred_element_type=F32)
    vT_ref[0] = vT.reshape(N_HEADS, V_DIM, vT.shape[-1]).astype(BF16)


def _pre_call(layer, x, p, tabs):
    b, s, d = x.shape
    cosT, sinT, cosF, sinF = tabs
    half = ROPE // 2
    wspec = lambda shape: pl.BlockSpec((1,) + shape, lambda i, j, l: (l[0],) + (0,) * len(shape))
    grid_spec = pltpu.PrefetchScalarGridSpec(
        num_scalar_prefetch=1,
        grid=(b, s // TS_PRE),
        in_specs=[pl.BlockSpec((1, TS_PRE, d), lambda i, j, l: (i, j, 0)),
                  wspec((1, d)), wspec((d, W1_COLS)), wspec((1, Q_RANK)), wspec((1, KV_RANK)),
                  wspec((N_HEADS * HEAD_PAD, Q_RANK)), wspec((KV_RANK, N_HEADS * HEAD_PAD)),
                  wspec((N_HEADS * V_DIM, KV_RANK)),
                  wspec((HEAD_PAD, 1)), wspec((HEAD_PAD, 1)),
                  wspec((1, HEAD_PAD)), wspec((1, HEAD_PAD)), wspec((1, HEAD_PAD)),
                  pl.BlockSpec((1, half, TS_PRE), lambda i, j, l: (i, 0, j)),
                  pl.BlockSpec((1, half, TS_PRE), lambda i, j, l: (i, 0, j)),
                  pl.BlockSpec((1, TS_PRE, HEAD_PAD), lambda i, j, l: (i, j, 0)),
                  pl.BlockSpec((1, TS_PRE, HEAD_PAD), lambda i, j, l: (i, j, 0))],
        out_specs=[pl.BlockSpec((1, N_HEADS, HEAD_PAD, TS_PRE), lambda i, j, l: (i, 0, 0, j)),
                   pl.BlockSpec((1, N_HEADS, TS_PRE, HEAD_PAD), lambda i, j, l: (i, 0, j, 0)),
                   pl.BlockSpec((1, N_HEADS, V_DIM, TS_PRE), lambda i, j, l: (i, 0, 0, j))],
    )
    return pl.pallas_call(
        _pre_kernel,
        grid_spec=grid_spec,
        out_shape=[jax.ShapeDtypeStruct((b, N_HEADS, HEAD_PAD, s), BF16),
                   jax.ShapeDtypeStruct((b, N_HEADS, s, HEAD_PAD), BF16),
                   jax.ShapeDtypeStruct((b, N_HEADS, V_DIM, s), BF16)],
        compiler_params=pltpu.CompilerParams(dimension_semantics=("parallel", "parallel"),
                                             vmem_limit_bytes=VMEM_LIMIT),
        name="pre_attn",
    )(layer, x, p["norm_g"], p["w1"], p["q_norm_g"], p["kv_norm_g"], p["w_uqT"], p["w_uk"], p["w_uvT"],
      p["qg"], p["qoff"], p["gn"], p["ga"], p["gb"], cosT, sinT, cosF, sinF)


def _attn_kernel(qT_ref, k_ref, vT_ref, oT_ref, s_scr):
    qT = qT_ref[0, 0]
    tq = qT.shape[1]
    n_groups = k_ref.shape[2] // TK

    def scores(g, slot):
        start = pl.multiple_of(g * TK, TK)
        s = jnp.dot(k_ref[0, 0, pl.ds(start, TK), :], qT, preferred_element_type=F32)
        s_scr[slot] = s
        return jnp.max(s, axis=0, keepdims=True)

    def accumulate(g, slot, m, l, acc, m_grp):
        start = pl.multiple_of(g * TK, TK)
        m_new = jnp.maximum(m, m_grp)
        alpha = jnp.exp2(m - m_new)
        p = jnp.exp2(s_scr[slot] - m_new)
        l = l * alpha + jnp.sum(p, axis=0, keepdims=True)
        pv = jnp.dot(vT_ref[0, 0, :, pl.ds(start, TK)], p.astype(BF16), preferred_element_type=F32)
        return m_new, l, acc * alpha + pv

    def pair(i, carry):
        m, l, acc, m_grp = carry
        g = 2 * i
        m_grp1 = scores(g + 1, 1)
        m, l, acc = accumulate(g, 0, m, l, acc, m_grp)
        m_grp2 = scores(g + 2, 0)
        m, l, acc = accumulate(g + 1, 1, m, l, acc, m_grp1)
        return m, l, acc, m_grp2

    init = (jnp.full((1, tq), -1e30, F32), jnp.zeros((1, tq), F32), jnp.zeros((V_DIM, tq), F32), scores(0, 0))
    m, l, acc, m_grp = lax.fori_loop(0, n_groups // 2 - 1, pair, init)
    g = n_groups - 2
    m_grp1 = scores(g + 1, 1)
    m, l, acc = accumulate(g, 0, m, l, acc, m_grp)
    m, l, acc = accumulate(g + 1, 1, m, l, acc, m_grp1)
    oT_ref[0, 0] = acc / l


def _attn_bounded_kernel(qT_ref, k_ref, vT_ref, oT_ref, s_scr):
    qT = qT_ref[0, 0]
    tq = qT.shape[1]
    n_chunks = k_ref.shape[2] // TKB

    def scores(c, slot):
        start = pl.multiple_of(c * TKB, TKB)
        s_scr[slot] = jnp.dot(k_ref[0, 0, pl.ds(start, TKB), :], qT, preferred_element_type=F32)

    def consume(c, slot, l8, acc):
        start = pl.multiple_of(c * TKB, TKB)
        p = jnp.exp2(s_scr[slot])
        l8 = l8 + jnp.sum(p.reshape(TKB // SUBLANES, SUBLANES, tq), axis=0)
        acc = acc + jnp.dot(vT_ref[0, 0, :, pl.ds(start, TKB)], p.astype(BF16), preferred_element_type=F32)
        return l8, acc

    def trip(i, carry, last=False):
        l8, acc = carry
        for u in range(CHUNKS_B):
            c = i * CHUNKS_B + u
            if not (last and u == CHUNKS_B - 1):
                scores(c + 1, (u + 1) % 2)
            l8, acc = consume(c, u % 2, l8, acc)
        return l8, acc

    scores(0, 0)
    init = (jnp.zeros((SUBLANES, tq), F32), jnp.zeros((V_DIM, tq), F32))
    n_trips = n_chunks // CHUNKS_B
    carry = lax.fori_loop(0, n_trips - 1, trip, init)
    l8, acc = trip(n_trips - 1, carry, last=True)
    oT_ref[0, 0] = acc / jnp.sum(l8, axis=0, keepdims=True)


def _attn_bounded_call(qT, k, vT):
    b, nh, _, s = qT.shape
    assert CHUNKS_B % 2 == 0 and s % (CHUNKS_B * TKB) == 0
    return pl.pallas_call(
        _attn_bounded_kernel,
        grid=(b, nh, s // TQB),
        in_specs=[pl.BlockSpec((1, 1, HEAD_PAD, TQB), lambda i, h, j: (i, h, 0, j)),
                  pl.BlockSpec((1, 1, s, HEAD_PAD), lambda i, h, j: (i, h, 0, 0)),
                  pl.BlockSpec((1, 1, V_DIM, s), lambda i, h, j: (i, h, 0, 0))],
        out_specs=pl.BlockSpec((1, 1, V_DIM, TQB), lambda i, h, j: (i, h, 0, j)),
        out_shape=jax.ShapeDtypeStruct((b, nh, V_DIM, s), F32),
        scratch_shapes=[pltpu.VMEM((2, TKB, TQB), F32)],
        compiler_params=pltpu.CompilerParams(dimension_semantics=("parallel", "parallel", "arbitrary"),
                                             vmem_limit_bytes=VMEM_LIMIT),
        name="mla_attn_bounded",
    )(qT, k, vT)


def _attn_call(qT, k, vT):
    b, nh, _, s = qT.shape
    assert (s // TK) % 2 == 0
    return pl.pallas_call(
        _attn_kernel,
        grid=(b, nh, s // TQ),
        in_specs=[pl.BlockSpec((1, 1, HEAD_PAD, TQ), lambda i, h, j: (i, h, 0, j)),
                  pl.BlockSpec((1, 1, s, HEAD_PAD), lambda i, h, j: (i, h, 0, 0)),
                  pl.BlockSpec((1, 1, V_DIM, s), lambda i, h, j: (i, h, 0, 0))],
        out_specs=pl.BlockSpec((1, 1, V_DIM, TQ), lambda i, h, j: (i, h, 0, j)),
        out_shape=jax.ShapeDtypeStruct((b, nh, V_DIM, s), F32),
        scratch_shapes=[pltpu.VMEM((2, TK, TQ), F32)],
        compiler_params=pltpu.CompilerParams(dimension_semantics=("parallel", "parallel", "arbitrary"),
                                             vmem_limit_bytes=VMEM_LIMIT),
        name="mla_attn",
    )(qT, k, vT)


def _post_kernel(layer_ref, x_ref, xp_ref, xn_ref, oT_ref, mk_ref, mv_ref, ng_ref, wr_ref, bg_ref,
                 cw_ref, cb_ref, mqg_ref, wa_ref, wc_ref, wm_ref, wo_ref, out_ref):
    del layer_ref
    j = pl.program_id(1)
    nj = pl.num_programs(1)
    x = x_ref[0]
    t = x.shape[0]
    ng = ng_ref[0]
    hb = (_rms(x) * ng).astype(BF16)
    xh = jnp.concatenate([xp_ref[0, 0], xn_ref[0, 0]], axis=0)
    hh = (_rms(xh) * ng).astype(BF16)
    h_ext = jnp.concatenate([hb, hh], axis=0)

    def proj(lhs, lo, hi):
        return jnp.dot(lhs, wr_ref[0, :, lo:hi], preferred_element_type=F32)

    ccu = proj(h_ext, R_CC, R_QM)
    z_ext = ccu[:, :CONV_W] * ccu[:, CONV_W:]
    z = z_ext[:t]
    z_before = z_ext[t + HALO - 1:t + HALO] * (j > 0).astype(F32)
    z_after = z_ext[t + HALO:t + HALO + 1] * (j < nj - 1).astype(F32)
    row = lax.broadcasted_iota(jnp.int32, z.shape, 0)
    z_prev = jnp.where(row == 0, z_before, pltpu.roll(z, 1, 0))
    z_next = jnp.where(row == t - 1, z_after, pltpu.roll(z, t - 1, 0))
    cw = cw_ref[0]
    conv = z_prev * cw[0:1] + z * cw[1:2] + z_next * cw[2:3] + cb_ref[0]
    o_conv = proj(hb, R_CB, R_CC) * conv * _silu(proj(hb, R_GC, R_GM))
    y_conv = jnp.dot(o_conv.astype(BF16), wc_ref[0], preferred_element_type=F32)

    oT = oT_ref[0]
    o_attn = oT.reshape(MLA_W, t).T * _silu(proj(hb, R_GA, R_GC))
    y_attn = jnp.dot(o_attn.astype(BF16), wa_ref[0], preferred_element_type=F32)

    qm = proj(hb, R_QM, R_GA)
    mqg = mqg_ref[0]
    heads = []
    for h in range(MEM_HEADS):
        qh = (_rms(qm[:, h * MEM_HD:(h + 1) * MEM_HD]) * mqg).astype(BF16)
        s = lax.dot_general(qh, mk_ref[0, 0, h], NT_DIMS, preferred_element_type=F32)
        s = s * (MEM_HD ** -0.5)
        p = jnp.exp(s - jnp.max(s, axis=-1, keepdims=True))
        l = jnp.sum(p, axis=-1, keepdims=True)
        oh = jnp.dot(p.astype(BF16), mv_ref[0, 0, h], preferred_element_type=F32)
        heads.append(oh / l)
    o_mem = jnp.concatenate(heads, axis=-1) * _silu(proj(hb, R_GM, R_R))
    y_mem = jnp.dot(o_mem.astype(BF16), wm_ref[0], preferred_element_type=F32)

    bg = bg_ref[0]
    r_a = _sigmoid(proj(hb, R_R, R_R + D_MODEL) + bg[:, :D_MODEL])
    y = r_a * y_attn
    r_c = _sigmoid(proj(hb, R_R + D_MODEL, R_R + 2 * D_MODEL) + bg[:, D_MODEL:2 * D_MODEL])
    y = y + r_c * y_conv
    r_m = _sigmoid(proj(hb, R_R + 2 * D_MODEL, R_R + 3 * D_MODEL) + bg[:, 2 * D_MODEL:])
    y = y + r_m * y_mem
    out_ref[0] = x + jnp.dot(y.astype(BF16), wo_ref[0], preferred_element_type=F32)


def _post_call(layer, x, oT, mk, mv, p):
    b, s, d = x.shape
    m = mk.shape[3]
    nblk = TS_POST // HALO
    last = s // HALO - 1
    x4 = x.reshape(b, s // HALO, HALO, d)
    wspec = lambda shape: pl.BlockSpec((1,) + shape, lambda i, j, l: (l[0],) + (0,) * len(shape))
    grid_spec = pltpu.PrefetchScalarGridSpec(
        num_scalar_prefetch=1,
        grid=(b, s // TS_POST),
        in_specs=[pl.BlockSpec((1, TS_POST, d), lambda i, j, l: (i, j, 0)),
                  pl.BlockSpec((1, 1, HALO, d), lambda i, j, l: (i, jnp.maximum(j * nblk - 1, 0), 0, 0)),
                  pl.BlockSpec((1, 1, HALO, d), lambda i, j, l: (i, jnp.minimum((j + 1) * nblk, last), 0, 0)),
                  pl.BlockSpec((1, N_HEADS, V_DIM, TS_POST), lambda i, j, l: (i, 0, 0, j)),
                  pl.BlockSpec((1, 1, MEM_HEADS, m, MEM_HD), lambda i, j, l: (l[0], i, 0, 0, 0)),
                  pl.BlockSpec((1, 1, MEM_HEADS, m, MEM_HD), lambda i, j, l: (l[0], i, 0, 0, 0)),
                  wspec((1, d)), wspec((d, REST_COLS)), wspec((1, N_BRANCH * d)),
                  wspec((3, CONV_W)), wspec((1, CONV_W)), wspec((1, MEM_HD)),
                  wspec((MLA_W, d)), wspec((CONV_W, d)), wspec((MEM_W, d)), wspec((d, d))],
        out_specs=pl.BlockSpec((1, TS_POST, d), lambda i, j, l: (i, j, 0)),
    )
    return pl.pallas_call(
        _post_kernel,
        grid_spec=grid_spec,
        out_shape=jax.ShapeDtypeStruct((b, s, d), F32),
        compiler_params=pltpu.CompilerParams(dimension_semantics=("parallel", "parallel"),
                                             vmem_limit_bytes=VMEM_LIMIT),
        name="post_attn",
    )(layer, x, x4, x4, oT, mk, mv, p["norm_g"], p["w_rest"], p["b_gate"], p["conv_w"], p["conv_b"],
      p["mem_q_g"], p["w_br_attn"], p["w_br_conv"], p["w_br_mem"], p["w_out"])


def _prepare_params(norm_g, w_in, b_gate, q_norm_g, w_uq, kv_norm_g, w_ukv, q_head_g, k_head_g,
                    conv_w, conv_b, mem_q_g, w_br_attn, w_br_conv, w_br_mem, w_out):
    depth = w_in.shape[0]
    d = D_MODEL
    half = ROPE // 2
    z = lambda n: jnp.zeros((depth, d, n), F32)
    t1 = w_in[:, :, OFF_KPE:OFF_KPE + half]
    t2 = w_in[:, :, OFF_KPE + half:OFF_REST]
    pad = HEAD_PAD - QK_DIM
    w1 = jnp.concatenate([w_in[:, :, :OFF_KPE], z(NOPE), t1, t2, z(pad), z(NOPE), t2, t1, z(pad)], axis=-1)

    w_uq4 = w_uq.reshape(depth, Q_RANK, N_HEADS, QK_DIM)
    w_uq_pad = jnp.pad(w_uq4, ((0, 0), (0, 0), (0, 0), (0, pad))).reshape(depth, Q_RANK, N_HEADS * HEAD_PAD)
    w_ukv4 = w_ukv.reshape(depth, KV_RANK, N_HEADS, NOPE + V_DIM)
    w_uk = jnp.pad(w_ukv4[..., :NOPE], ((0, 0), (0, 0), (0, 0), (0, HEAD_PAD - NOPE)))
    w_uv = w_ukv4[..., NOPE:].reshape(depth, KV_RANK, N_HEADS * V_DIM)

    zg = lambda n: jnp.zeros((depth, n), F32)
    q_scale = (QK_DIM ** -0.5) * LOG2E
    qg = jnp.concatenate([q_head_g * q_scale, zg(pad)], axis=-1)[..., None]
    score_bound = (QK_DIM ** 0.5) * LOG2E * jnp.max(jnp.abs(q_head_g), axis=-1) * jnp.max(jnp.abs(k_head_g), axis=-1)
    bounded = score_bound <= MAX_SCORE_BOUND
    off = jnp.where(bounded, -score_bound, 0.0)
    qoff = jnp.concatenate([zg(QK_DIM), jnp.broadcast_to(off[:, None], (depth, pad))], axis=-1)[..., None]
    g_nope, g1, g2 = k_head_g[:, :NOPE], k_head_g[:, NOPE:NOPE + half], k_head_g[:, NOPE + half:]
    gn = jnp.concatenate([g_nope, zg(HEAD_PAD - NOPE)], axis=-1)[:, None, :]
    ga = jnp.concatenate([zg(NOPE), g1, g2, zg(pad)], axis=-1)[:, None, :]
    gb = jnp.concatenate([zg(NOPE), g2, g1, zg(pad)], axis=-1)[:, None, :]
    return dict(
        norm_g=norm_g[:, None, :],
        w1=w1.astype(BF16),
        q_norm_g=q_norm_g[:, None, :],
        kv_norm_g=kv_norm_g[:, None, :],
        w_uqT=jnp.swapaxes(w_uq_pad, 1, 2).astype(BF16),
        w_uk=w_uk.reshape(depth, KV_RANK, N_HEADS * HEAD_PAD).astype(BF16),
        w_uvT=jnp.swapaxes(w_uv, 1, 2).astype(BF16),
        qg=qg, qoff=qoff, bounded=bounded, gn=gn, ga=ga, gb=gb,
        w_rest=w_in[:, :, OFF_REST:].astype(BF16),
        b_gate=b_gate[:, None, :],
        conv_w=conv_w,
        conv_b=conv_b[:, None, :],
        mem_q_g=mem_q_g[:, None, :],
        w_br_attn=w_br_attn.astype(BF16),
        w_br_conv=w_br_conv.astype(BF16),
        w_br_mem=w_br_mem.astype(BF16),
        w_out=w_out.astype(BF16),
    )


def kernel(x, mem, positions, norm_g, w_in, b_gate, q_norm_g, w_uq, kv_norm_g, w_ukv, q_head_g, k_head_g,
           conv_w, conv_b, mem_norm_g, w_mkv, mem_q_g, mem_k_g, w_br_attn, w_br_conv, w_br_mem, w_out):
    depth = w_in.shape[0]
    assert x.shape[-1] == D_MODEL and w_in.shape[-1] == OFF_REST + REST_COLS
    assert x.shape[1] % max(TS_PRE, TS_POST, TQ, TK, TS_TAB) == 0
    p = _prepare_params(norm_g, w_in, b_gate, q_norm_g, w_uq, kv_norm_g, w_ukv, q_head_g, k_head_g,
                        conv_w, conv_b, mem_q_g, w_br_attn, w_br_conv, w_br_mem, w_out)
    invf = ROPE_BASE ** (-jnp.arange(0, ROPE, 2, dtype=F32) / ROPE)
    tabs = _rope_tables(positions, invf[:, None])
    mk, mv = _mem_kv(mem, mem_norm_g, w_mkv.astype(BF16), mem_k_g)

    for i in range(depth):
        layer = jnp.full((1,), i, jnp.int32)
        qT, k, vT = _pre_call(layer, x, p, tabs)
        oT = lax.cond(p["bounded"][i], _attn_bounded_call, _attn_call, qT, k, vT)
        x = _post_call(layer, x, oT, mk, mv, p)
    return x
```

```python
import functools
import math

import jax
import jax.numpy as jnp
from jax import lax
from jax.experimental import pallas as pl
from jax.experimental.pallas import tpu as pltpu

D_MODEL = 1024
N_HEADS = 8
NOPE = 64
ROPE = 32
QK_DIM = NOPE + ROPE
V_DIM = 64
Q_RANK = 3 * D_MODEL // 8
KV_RANK = D_MODEL // 4
MLA_W = N_HEADS * V_DIM
CONV_W = D_MODEL // 2
MEM_HEADS = 4
MEM_HD = 128
MEM_W = MEM_HEADS * MEM_HD
N_BRANCH = 3
ROPE_BASE = 10000.0
EPS = 1e-6
LOG2E = math.log2(math.e)

LANES = 128
SUBLANES = 8
HEAD_PAD = LANES
VMEM_LIMIT = 56 * 1024 * 1024

OFF_KVLAT = Q_RANK
OFF_KPE = Q_RANK + KV_RANK
OFF_REST = OFF_KPE + ROPE
W1_COLS = OFF_KPE + 2 * HEAD_PAD
R_CB, R_CC, R_CU, R_QM, R_GA, R_GC, R_GM, R_R = (0, 512, 1024, 1536, 2048, 2560, 3072, 3584)
REST_COLS = R_R + N_BRANCH * D_MODEL

TS_TAB = 512
TS_PRE = 512
TS_POST = 256
TQ = 256
TK = 1024
HALO = SUBLANES
TQB = 1024
TKB = 256
MAX_SCORE_BOUND = 48.0

F32 = jnp.float32
BF16 = jnp.bfloat16
NT_DIMS = (((1,), (1,)), ((), ()))


def _rms(t, axis=-1):
    return t * lax.rsqrt(jnp.mean(t * t, axis=axis, keepdims=True) + EPS)


def _sigmoid(t):
    return 1.0 / (1.0 + jnp.exp(-t))


def _silu(t):
    return t * _sigmoid(t)


def _tables_kernel(pos_ref, invf_ref, cosT_ref, sinT_ref, cosF_ref, sinF_ref):
    pos = pos_ref[0].astype(F32)
    ang = invf_ref[...] * pos
    c = jnp.cos(ang)
    s = jnp.sin(ang)
    cosT_ref[0] = c
    sinT_ref[0] = s
    t = pos.shape[1]
    z64 = jnp.zeros((NOPE, t), F32)
    z32 = jnp.zeros((HEAD_PAD - QK_DIM, t), F32)
    cosF_ref[0] = jnp.concatenate([z64, c, c, z32], axis=0).T
    sinF_ref[0] = jnp.concatenate([z64, -s, s, z32], axis=0).T


def _rope_tables(positions, invf):
    b, s = positions.shape
    half = ROPE // 2
    return pl.pallas_call(
        _tables_kernel,
        grid=(b, s // TS_TAB),
        in_specs=[pl.BlockSpec((1, 1, TS_TAB), lambda i, j: (i, 0, j)),
                  pl.BlockSpec((half, 1), lambda i, j: (0, 0))],
        out_specs=[pl.BlockSpec((1, half, TS_TAB), lambda i, j: (i, 0, j)),
                   pl.BlockSpec((1, half, TS_TAB), lambda i, j: (i, 0, j)),
                   pl.BlockSpec((1, TS_TAB, HEAD_PAD), lambda i, j: (i, j, 0)),
                   pl.BlockSpec((1, TS_TAB, HEAD_PAD), lambda i, j: (i, j, 0))],
        out_shape=[jax.ShapeDtypeStruct((b, half, s), F32),
                   jax.ShapeDtypeStruct((b, half, s), F32),
                   jax.ShapeDtypeStruct((b, s, HEAD_PAD), F32),
                   jax.ShapeDtypeStruct((b, s, HEAD_PAD), F32)],
        compiler_params=pltpu.CompilerParams(dimension_semantics=("parallel", "parallel")),
        name="rope_tables",
    )(positions.reshape(b, 1, s), invf)


def _mem_kernel(mem_ref, g_ref, w_ref, kg_ref, mk_ref, mv_ref):
    m = mem_ref[0]
    mn = (_rms(m) * g_ref[0]).astype(BF16)
    mkv = jnp.dot(mn, w_ref[0], preferred_element_type=F32)
    for h in range(MEM_HEADS):
        kh = mkv[:, h * 2 * MEM_HD: h * 2 * MEM_HD + MEM_HD]
        vh = mkv[:, h * 2 * MEM_HD + MEM_HD: (h + 1) * 2 * MEM_HD]
        mk_ref[0, 0, h] = (_rms(kh) * kg_ref[0]).astype(BF16)
        mv_ref[0, 0, h] = vh.astype(BF16)


def _mem_kv(mem, mem_norm_g, w_mkv_bf, mem_k_g):
    depth = w_mkv_bf.shape[0]
    b, m, d = mem.shape
    return pl.pallas_call(
        _mem_kernel,
        grid=(depth, b),
        in_specs=[pl.BlockSpec((1, m, d), lambda l, i: (i, 0, 0)),
                  pl.BlockSpec((1, 1, d), lambda l, i: (l, 0, 0)),
                  pl.BlockSpec((1, d, 2 * MEM_W), lambda l, i: (l, 0, 0)),
                  pl.BlockSpec((1, 1, MEM_HD), lambda l, i: (l, 0, 0))],
        out_specs=[pl.BlockSpec((1, 1, MEM_HEADS, m, MEM_HD), lambda l, i: (l, i, 0, 0, 0)),
                   pl.BlockSpec((1, 1, MEM_HEADS, m, MEM_HD), lambda l, i: (l, i, 0, 0, 0))],
        out_shape=[jax.ShapeDtypeStruct((depth, b, MEM_HEADS, m, MEM_HD), BF16),
                   jax.ShapeDtypeStruct((depth, b, MEM_HEADS, m, MEM_HD), BF16)],
        compiler_params=pltpu.CompilerParams(dimension_semantics=("parallel", "parallel")),
        name="mem_kv",
    )(mem, mem_norm_g.reshape(depth, 1, d), w_mkv_bf, mem_k_g.reshape(depth, 1, MEM_HD))


def _pre_kernel(layer_ref, x_ref, ng_ref, w1_ref, qng_ref, kvng_ref, wuq_ref, wuk_ref, wuv_ref,
                qg_ref, qoff_ref, gn_ref, ga_ref, gb_ref, cosT_ref, sinT_ref, cosF_ref, sinF_ref,
                qT_ref, k_ref, vT_ref):
    del layer_ref
    x = x_ref[0]
    hb = (_rms(x) * ng_ref[0]).astype(BF16)
    a = jnp.dot(hb, w1_ref[0], preferred_element_type=F32)
    qn = (_rms(a[:, :Q_RANK]) * qng_ref[0]).astype(BF16)
    kvn = (_rms(a[:, OFF_KVLAT:OFF_KPE]) * kvng_ref[0]).astype(BF16)
    pe = a[:, OFF_KPE:OFF_KPE + HEAD_PAD]
    pe_sw = a[:, OFF_KPE + HEAD_PAD:W1_COLS]

    qT = lax.dot_general(wuq_ref[0], qn, NT_DIMS, preferred_element_type=F32)
    cT = cosT_ref[0]
    sT = sinT_ref[0]
    qg = qg_ref[0]
    qoff = qoff_ref[0]
    half = ROPE // 2
    for h in range(N_HEADS):
        qh = qT[h * HEAD_PAD:(h + 1) * HEAD_PAD]
        ms = jnp.sum(qh * qh, axis=0, keepdims=True) * (1.0 / QK_DIM)
        qh = qh * lax.rsqrt(ms + EPS) * qg + qoff
        t1 = qh[NOPE:NOPE + half]
        t2 = qh[NOPE + half:QK_DIM]
        qh = jnp.concatenate([qh[:NOPE], t1 * cT - t2 * sT, t2 * cT + t1 * sT, qh[QK_DIM:]], axis=0)
        qT_ref[0, h] = qh.astype(BF16)

    knope = jnp.dot(kvn, wuk_ref[0], preferred_element_type=F32)
    u = pe * (ga_ref[0] * cosF_ref[0]) + pe_sw * (gb_ref[0] * sinF_ref[0])
    ss_pe = jnp.sum(pe * pe, axis=-1, keepdims=True)
    gn = gn_ref[0]
    one_lane = (lax.broadcasted_iota(jnp.int32, (1, HEAD_PAD), 1) == QK_DIM).astype(F32)
    for h in range(N_HEADS):
        kn = knope[:, h * HEAD_PAD:(h + 1) * HEAD_PAD]
        ms = (jnp.sum(kn * kn, axis=-1, keepdims=True) + ss_pe) * (1.0 / QK_DIM)
        k_ref[0, h] = ((kn * gn + u) * lax.rsqrt(ms + EPS) + one_lane).astype(BF16)

    vT = lax.dot_general(wuv_ref[0], kvn, NT_DIMS, preferred_element_type=F32)
    vT_ref[0] = vT.reshape(N_HEADS, V_DIM, vT.shape[-1]).astype(BF16)


def _pre_call(layer, x, p, tabs):
    b, s, d = x.shape
    cosT, sinT, cosF, sinF = tabs
    half = ROPE // 2
    wspec = lambda shape: pl.BlockSpec((1,) + shape, lambda i, j, l: (l[0],) + (0,) * len(shape))
    grid_spec = pltpu.PrefetchScalarGridSpec(
        num_scalar_prefetch=1,
        grid=(b, s // TS_PRE),
        in_specs=[pl.BlockSpec((1, TS_PRE, d), lambda i, j, l: (i, j, 0)),
                  wspec((1, d)), wspec((d, W1_COLS)), wspec((1, Q_RANK)), wspec((1, KV_RANK)),
                  wspec((N_HEADS * HEAD_PAD, Q_RANK)), wspec((KV_RANK, N_HEADS * HEAD_PAD)),
                  wspec((N_HEADS * V_DIM, KV_RANK)),
                  wspec((HEAD_PAD, 1)), wspec((HEAD_PAD, 1)),
                  wspec((1, HEAD_PAD)), wspec((1, HEAD_PAD)), wspec((1, HEAD_PAD)),
                  pl.BlockSpec((1, half, TS_PRE), lambda i, j, l: (i, 0, j)),
                  pl.BlockSpec((1, half, TS_PRE), lambda i, j, l: (i, 0, j)),
                  pl.BlockSpec((1, TS_PRE, HEAD_PAD), lambda i, j, l: (i, j, 0)),
                  pl.BlockSpec((1, TS_PRE, HEAD_PAD), lambda i, j, l: (i, j, 0))],
        out_specs=[pl.BlockSpec((1, N_HEADS, HEAD_PAD, TS_PRE), lambda i, j, l: (i, 0, 0, j)),
                   pl.BlockSpec((1, N_HEADS, TS_PRE, HEAD_PAD), lambda i, j, l: (i, 0, j, 0)),
                   pl.BlockSpec((1, N_HEADS, V_DIM, TS_PRE), lambda i, j, l: (i, 0, 0, j))],
    )
    return pl.pallas_call(
        _pre_kernel,
        grid_spec=grid_spec,
        out_shape=[jax.ShapeDtypeStruct((b, N_HEADS, HEAD_PAD, s), BF16),
                   jax.ShapeDtypeStruct((b, N_HEADS, s, HEAD_PAD), BF16),
                   jax.ShapeDtypeStruct((b, N_HEADS, V_DIM, s), BF16)],
        compiler_params=pltpu.CompilerParams(dimension_semantics=("parallel", "parallel"),
                                             vmem_limit_bytes=VMEM_LIMIT),
        name="pre_attn",
    )(layer, x, p["norm_g"], p["w1"], p["q_norm_g"], p["kv_norm_g"], p["w_uqT"], p["w_uk"], p["w_uvT"],
      p["qg"], p["qoff"], p["gn"], p["ga"], p["gb"], cosT, sinT, cosF, sinF)


def _attn_kernel(qT_ref, k_ref, vT_ref, oT_ref, s_scr):
    qT = qT_ref[0, 0]
    tq = qT.shape[1]
    n_groups = k_ref.shape[2] // TK

    def scores(g, slot):
        start = pl.multiple_of(g * TK, TK)
        s = jnp.dot(k_ref[0, 0, pl.ds(start, TK), :], qT, preferred_element_type=F32)
        s_scr[slot] = s
        return jnp.max(s, axis=0, keepdims=True)

    def accumulate(g, slot, m, l, acc, m_grp):
        start = pl.multiple_of(g * TK, TK)
        m_new = jnp.maximum(m, m_grp)
        alpha = jnp.exp2(m - m_new)
        p = jnp.exp2(s_scr[slot] - m_new)
        l = l * alpha + jnp.sum(p, axis=0, keepdims=True)
        pv = jnp.dot(vT_ref[0, 0, :, pl.ds(start, TK)], p.astype(BF16), preferred_element_type=F32)
        return m_new, l, acc * alpha + pv

    def pair(i, carry):
        m, l, acc, m_grp = carry
        g = 2 * i
        m_grp1 = scores(g + 1, 1)
        m, l, acc = accumulate(g, 0, m, l, acc, m_grp)
        m_grp2 = scores(g + 2, 0)
        m, l, acc = accumulate(g + 1, 1, m, l, acc, m_grp1)
        return m, l, acc, m_grp2

    init = (jnp.full((1, tq), -1e30, F32), jnp.zeros((1, tq), F32), jnp.zeros((V_DIM, tq), F32), scores(0, 0))
    m, l, acc, m_grp = lax.fori_loop(0, n_groups // 2 - 1, pair, init)
    g = n_groups - 2
    m_grp1 = scores(g + 1, 1)
    m, l, acc = accumulate(g, 0, m, l, acc, m_grp)
    m, l, acc = accumulate(g + 1, 1, m, l, acc, m_grp1)
    oT_ref[0, 0] = acc / l


def _attn_bounded_kernel(qT_ref, k_ref, vT_ref, oT_ref):
    qT = qT_ref[0, 0]
    tq = qT.shape[1]
    n_chunks = k_ref.shape[2] // TKB

    def scores(c):
        return jnp.dot(k_ref[0, 0, c * TKB:(c + 1) * TKB, :], qT, preferred_element_type=F32)

    l8 = jnp.zeros((SUBLANES, tq), F32)
    acc = jnp.zeros((V_DIM, tq), F32)
    s_next = scores(0)
    for c in range(n_chunks):
        s_cur = s_next
        if c + 1 < n_chunks:
            s_next = scores(c + 1)
        p = jnp.exp2(s_cur)
        l8 = l8 + jnp.sum(p.reshape(TKB // SUBLANES, SUBLANES, tq), axis=0)
        acc = acc + jnp.dot(vT_ref[0, 0, :, c * TKB:(c + 1) * TKB], p.astype(BF16),
                            preferred_element_type=F32)
    oT_ref[0, 0] = acc / jnp.sum(l8, axis=0, keepdims=True)


def _attn_bounded_call(qT, k, vT):
    b, nh, _, s = qT.shape
    assert s % TKB == 0 and s % TQB == 0
    return pl.pallas_call(
        _attn_bounded_kernel,
        grid=(b, nh, s // TQB),
        in_specs=[pl.BlockSpec((1, 1, HEAD_PAD, TQB), lambda i, h, j: (i, h, 0, j)),
                  pl.BlockSpec((1, 1, s, HEAD_PAD), lambda i, h, j: (i, h, 0, 0)),
                  pl.BlockSpec((1, 1, V_DIM, s), lambda i, h, j: (i, h, 0, 0))],
        out_specs=pl.BlockSpec((1, 1, V_DIM, TQB), lambda i, h, j: (i, h, 0, j)),
        out_shape=jax.ShapeDtypeStruct((b, nh, V_DIM, s), F32),
        compiler_params=pltpu.CompilerParams(dimension_semantics=("parallel", "parallel", "arbitrary"),
                                             vmem_limit_bytes=VMEM_LIMIT),
        name="mla_attn_bounded",
    )(qT, k, vT)


def _attn_call(qT, k, vT):
    b, nh, _, s = qT.shape
    assert (s // TK) % 2 == 0
    return pl.pallas_call(
        _attn_kernel,
        grid=(b, nh, s // TQ),
        in_specs=[pl.BlockSpec((1, 1, HEAD_PAD, TQ), lambda i, h, j: (i, h, 0, j)),
                  pl.BlockSpec((1, 1, s, HEAD_PAD), lambda i, h, j: (i, h, 0, 0)),
                  pl.BlockSpec((1, 1, V_DIM, s), lambda i, h, j: (i, h, 0, 0))],
        out_specs=pl.BlockSpec((1, 1, V_DIM, TQ), lambda i, h, j: (i, h, 0, j)),
        out_shape=jax.ShapeDtypeStruct((b, nh, V_DIM, s), F32),
        scratch_shapes=[pltpu.VMEM((2, TK, TQ), F32)],
        compiler_params=pltpu.CompilerParams(dimension_semantics=("parallel", "parallel", "arbitrary"),
                                             vmem_limit_bytes=VMEM_LIMIT),
        name="mla_attn",
    )(qT, k, vT)


def _post_kernel(layer_ref, x_ref, xp_ref, xn_ref, oT_ref, mk_ref, mv_ref, ng_ref, wr_ref, bg_ref,
                 cw_ref, cb_ref, mqg_ref, wa_ref, wc_ref, wm_ref, wo_ref, out_ref):
    del layer_ref
    j = pl.program_id(1)
    nj = pl.num_programs(1)
    x = x_ref[0]
    t = x.shape[0]
    ng = ng_ref[0]
    hb = (_rms(x) * ng).astype(BF16)
    xh = jnp.concatenate([xp_ref[0, 0], xn_ref[0, 0]], axis=0)
    hh = (_rms(xh) * ng).astype(BF16)
    h_ext = jnp.concatenate([hb, hh], axis=0)

    def proj(lhs, lo, hi):
        return jnp.dot(lhs, wr_ref[0, :, lo:hi], preferred_element_type=F32)

    ccu = proj(h_ext, R_CC, R_QM)
    z_ext = ccu[:, :CONV_W] * ccu[:, CONV_W:]
    z = z_ext[:t]
    z_before = z_ext[t + HALO - 1:t + HALO] * (j > 0).astype(F32)
    z_after = z_ext[t + HALO:t + HALO + 1] * (j < nj - 1).astype(F32)
    row = lax.broadcasted_iota(jnp.int32, z.shape, 0)
    z_prev = jnp.where(row == 0, z_before, pltpu.roll(z, 1, 0))
    z_next = jnp.where(row == t - 1, z_after, pltpu.roll(z, t - 1, 0))
    cw = cw_ref[0]
    conv = z_prev * cw[0:1] + z * cw[1:2] + z_next * cw[2:3] + cb_ref[0]
    o_conv = proj(hb, R_CB, R_CC) * conv * _silu(proj(hb, R_GC, R_GM))
    y_conv = jnp.dot(o_conv.astype(BF16), wc_ref[0], preferred_element_type=F32)

    oT = oT_ref[0]
    o_attn = oT.reshape(MLA_W, t).T * _silu(proj(hb, R_GA, R_GC))
    y_attn = jnp.dot(o_attn.astype(BF16), wa_ref[0], preferred_element_type=F32)

    qm = proj(hb, R_QM, R_GA)
    mqg = mqg_ref[0]
    heads = []
    for h in range(MEM_HEADS):
        qh = (_rms(qm[:, h * MEM_HD:(h + 1) * MEM_HD]) * mqg).astype(BF16)
        s = lax.dot_general(qh, mk_ref[0, 0, h], NT_DIMS, preferred_element_type=F32)
        s = s * (MEM_HD ** -0.5)
        p = jnp.exp(s - jnp.max(s, axis=-1, keepdims=True))
        l = jnp.sum(p, axis=-1, keepdims=True)
        oh = jnp.dot(p.astype(BF16), mv_ref[0, 0, h], preferred_element_type=F32)
        heads.append(oh / l)
    o_mem = jnp.concatenate(heads, axis=-1) * _silu(proj(hb, R_GM, R_R))
    y_mem = jnp.dot(o_mem.astype(BF16), wm_ref[0], preferred_element_type=F32)

    bg = bg_ref[0]
    r_a = _sigmoid(proj(hb, R_R, R_R + D_MODEL) + bg[:, :D_MODEL])
    y = r_a * y_attn
    r_c = _sigmoid(proj(hb, R_R + D_MODEL, R_R + 2 * D_MODEL) + bg[:, D_MODEL:2 * D_MODEL])
    y = y + r_c * y_conv
    r_m = _sigmoid(proj(hb, R_R + 2 * D_MODEL, R_R + 3 * D_MODEL) + bg[:, 2 * D_MODEL:])
    y = y + r_m * y_mem
    out_ref[0] = x + jnp.dot(y.astype(BF16), wo_ref[0], preferred_element_type=F32)


def _post_call(layer, x, oT, mk, mv, p):
    b, s, d = x.shape
    m = mk.shape[3]
    nblk = TS_POST // HALO
    last = s // HALO - 1
    x4 = x.reshape(b, s // HALO, HALO, d)
    wspec = lambda shape: pl.BlockSpec((1,) + shape, lambda i, j, l: (l[0],) + (0,) * len(shape))
    grid_spec = pltpu.PrefetchScalarGridSpec(
        num_scalar_prefetch=1,
        grid=(b, s // TS_POST),
        in_specs=[pl.BlockSpec((1, TS_POST, d), lambda i, j, l: (i, j, 0)),
                  pl.BlockSpec((1, 1, HALO, d), lambda i, j, l: (i, jnp.maximum(j * nblk - 1, 0), 0, 0)),
                  pl.BlockSpec((1, 1, HALO, d), lambda i, j, l: (i, jnp.minimum((j + 1) * nblk, last), 0, 0)),
                  pl.BlockSpec((1, N_HEADS, V_DIM, TS_POST), lambda i, j, l: (i, 0, 0, j)),
                  pl.BlockSpec((1, 1, MEM_HEADS, m, MEM_HD), lambda i, j, l: (l[0], i, 0, 0, 0)),
                  pl.BlockSpec((1, 1, MEM_HEADS, m, MEM_HD), lambda i, j, l: (l[0], i, 0, 0, 0)),
                  wspec((1, d)), wspec((d, REST_COLS)), wspec((1, N_BRANCH * d)),
                  wspec((3, CONV_W)), wspec((1, CONV_W)), wspec((1, MEM_HD)),
                  wspec((MLA_W, d)), wspec((CONV_W, d)), wspec((MEM_W, d)), wspec((d, d))],
        out_specs=pl.BlockSpec((1, TS_POST, d), lambda i, j, l: (i, j, 0)),
    )
    return pl.pallas_call(
        _post_kernel,
        grid_spec=grid_spec,
        out_shape=jax.ShapeDtypeStruct((b, s, d), F32),
        compiler_params=pltpu.CompilerParams(dimension_semantics=("parallel", "parallel"),
                                             vmem_limit_bytes=VMEM_LIMIT),
        name="post_attn",
    )(layer, x, x4, x4, oT, mk, mv, p["norm_g"], p["w_rest"], p["b_gate"], p["conv_w"], p["conv_b"],
      p["mem_q_g"], p["w_br_attn"], p["w_br_conv"], p["w_br_mem"], p["w_out"])


def _prepare_params(norm_g, w_in, b_gate, q_norm_g, w_uq, kv_norm_g, w_ukv, q_head_g, k_head_g,
                    conv_w, conv_b, mem_q_g, w_br_attn, w_br_conv, w_br_mem, w_out):
    depth = w_in.shape[0]
    d = D_MODEL
    half = ROPE // 2
    z = lambda n: jnp.zeros((depth, d, n), F32)
    t1 = w_in[:, :, OFF_KPE:OFF_KPE + half]
    t2 = w_in[:, :, OFF_KPE + half:OFF_REST]
    pad = HEAD_PAD - QK_DIM
    w1 = jnp.concatenate([w_in[:, :, :OFF_KPE], z(NOPE), t1, t2, z(pad), z(NOPE), t2, t1, z(pad)], axis=-1)

    w_uq4 = w_uq.reshape(depth, Q_RANK, N_HEADS, QK_DIM)
    w_uq_pad = jnp.pad(w_uq4, ((0, 0), (0, 0), (0, 0), (0, pad))).reshape(depth, Q_RANK, N_HEADS * HEAD_PAD)
    w_ukv4 = w_ukv.reshape(depth, KV_RANK, N_HEADS, NOPE + V_DIM)
    w_uk = jnp.pad(w_ukv4[..., :NOPE], ((0, 0), (0, 0), (0, 0), (0, HEAD_PAD - NOPE)))
    w_uv = w_ukv4[..., NOPE:].reshape(depth, KV_RANK, N_HEADS * V_DIM)

    zg = lambda n: jnp.zeros((depth, n), F32)
    q_scale = (QK_DIM ** -0.5) * LOG2E
    qg = jnp.concatenate([q_head_g * q_scale, zg(pad)], axis=-1)[..., None]
    score_bound = (QK_DIM ** 0.5) * LOG2E * jnp.max(jnp.abs(q_head_g), axis=-1) * jnp.max(jnp.abs(k_head_g), axis=-1)
    bounded = score_bound <= MAX_SCORE_BOUND
    off = jnp.where(bounded, -score_bound, 0.0)
    qoff = jnp.concatenate([zg(QK_DIM), jnp.broadcast_to(off[:, None], (depth, pad))], axis=-1)[..., None]
    g_nope, g1, g2 = k_head_g[:, :NOPE], k_head_g[:, NOPE:NOPE + half], k_head_g[:, NOPE + half:]
    gn = jnp.concatenate([g_nope, zg(HEAD_PAD - NOPE)], axis=-1)[:, None, :]
    ga = jnp.concatenate([zg(NOPE), g1, g2, zg(pad)], axis=-1)[:, None, :]
    gb = jnp.concatenate([zg(NOPE), g2, g1, zg(pad)], axis=-1)[:, None, :]
    return dict(
        norm_g=norm_g[:, None, :],
        w1=w1.astype(BF16),
        q_norm_g=q_norm_g[:, None, :],
        kv_norm_g=kv_norm_g[:, None, :],
        w_uqT=jnp.swapaxes(w_uq_pad, 1, 2).astype(BF16),
        w_uk=w_uk.reshape(depth, KV_RANK, N_HEADS * HEAD_PAD).astype(BF16),
        w_uvT=jnp.swapaxes(w_uv, 1, 2).astype(BF16),
        qg=qg, qoff=qoff, bounded=bounded, gn=gn, ga=ga, gb=gb,
        w_rest=w_in[:, :, OFF_REST:].astype(BF16),
        b_gate=b_gate[:, None, :],
        conv_w=conv_w,
        conv_b=conv_b[:, None, :],
        mem_q_g=mem_q_g[:, None, :],
        w_br_attn=w_br_attn.astype(BF16),
        w_br_conv=w_br_conv.astype(BF16),
        w_br_mem=w_br_mem.astype(BF16),
        w_out=w_out.astype(BF16),
    )


def kernel(x, mem, positions, norm_g, w_in, b_gate, q_norm_g, w_uq, kv_norm_g, w_ukv, q_head_g, k_head_g,
           conv_w, conv_b, mem_norm_g, w_mkv, mem_q_g, mem_k_g, w_br_attn, w_br_conv, w_br_mem, w_out):
    depth = w_in.shape[0]
    assert x.shape[-1] == D_MODEL and w_in.shape[-1] == OFF_REST + REST_COLS
    assert x.shape[1] % max(TS_PRE, TS_POST, TQ, TK, TS_TAB) == 0
    p = _prepare_params(norm_g, w_in, b_gate, q_norm_g, w_uq, kv_norm_g, w_ukv, q_head_g, k_head_g,
                        conv_w, conv_b, mem_q_g, w_br_attn, w_br_conv, w_br_mem, w_out)
    invf = ROPE_BASE ** (-jnp.arange(0, ROPE, 2, dtype=F32) / ROPE)
    tabs = _rope_tables(positions, invf[:, None])
    mk, mv = _mem_kv(mem, mem_norm_g, w_mkv.astype(BF16), mem_k_g)

    for i in range(depth):
        layer = jnp.full((1,), i, jnp.int32)
        qT, k, vT = _pre_call(layer, x, p, tabs)
        oT = lax.cond(p["bounded"][i], _attn_bounded_call, _attn_call, qT, k, vT)
        x = _post_call(layer, x, oT, mk, mv, p)
    return x
```

```python
import functools
import math

import jax
import jax.numpy as jnp
from jax import lax
from jax.experimental import pallas as pl
from jax.experimental.pallas import tpu as pltpu

D_MODEL = 1024
N_HEADS = 8
NOPE = 64
ROPE = 32
QK_DIM = NOPE + ROPE
V_DIM = 64
Q_RANK = 3 * D_MODEL // 8
KV_RANK = D_MODEL // 4
MLA_W = N_HEADS * V_DIM
CONV_W = D_MODEL // 2
MEM_HEADS = 4
MEM_HD = 128
MEM_W = MEM_HEADS * MEM_HD
N_BRANCH = 3
ROPE_BASE = 10000.0
EPS = 1e-6
LOG2E = math.log2(math.e)

LANES = 128
SUBLANES = 8
HEAD_PAD = LANES
VMEM_LIMIT = 56 * 1024 * 1024

OFF_KVLAT = Q_RANK
OFF_KPE = Q_RANK + KV_RANK
OFF_REST = OFF_KPE + ROPE
W1_COLS = OFF_KPE + 2 * HEAD_PAD
R_CB, R_CC, R_CU, R_QM, R_GA, R_GC, R_GM, R_R = (0, 512, 1024, 1536, 2048, 2560, 3072, 3584)
REST_COLS = R_R + N_BRANCH * D_MODEL

TS_TAB = 512
TS_PRE = 512
TS_POST = 512
TQ = 256
TK = 1024
HALO = SUBLANES
TQB = 1024
TKB = 256
MAX_SCORE_BOUND = 48.0

F32 = jnp.float32
BF16 = jnp.bfloat16
NT_DIMS = (((1,), (1,)), ((), ()))


def _rms(t, axis=-1):
    return t * lax.rsqrt(jnp.mean(t * t, axis=axis, keepdims=True) + EPS)


def _sigmoid(t):
    return 1.0 / (1.0 + jnp.exp(-t))


def _silu(t):
    return t * _sigmoid(t)


def _tables_kernel(pos_ref, invf_ref, cosT_ref, sinT_ref, cosF_ref, sinF_ref):
    pos = pos_ref[0].astype(F32)
    ang = invf_ref[...] * pos
    c = jnp.cos(ang)
    s = jnp.sin(ang)
    cosT_ref[0] = c
    sinT_ref[0] = s
    t = pos.shape[1]
    z64 = jnp.zeros((NOPE, t), F32)
    z32 = jnp.zeros((HEAD_PAD - QK_DIM, t), F32)
    cosF_ref[0] = jnp.concatenate([z64, c, c, z32], axis=0).T
    sinF_ref[0] = jnp.concatenate([z64, -s, s, z32], axis=0).T


def _rope_tables(positions, invf):
    b, s = positions.shape
    half = ROPE // 2
    return pl.pallas_call(
        _tables_kernel,
        grid=(b, s // TS_TAB),
        in_specs=[pl.BlockSpec((1, 1, TS_TAB), lambda i, j: (i, 0, j)),
                  pl.BlockSpec((half, 1), lambda i, j: (0, 0))],
        out_specs=[pl.BlockSpec((1, half, TS_TAB), lambda i, j: (i, 0, j)),
                   pl.BlockSpec((1, half, TS_TAB), lambda i, j: (i, 0, j)),
                   pl.BlockSpec((1, TS_TAB, HEAD_PAD), lambda i, j: (i, j, 0)),
                   pl.BlockSpec((1, TS_TAB, HEAD_PAD), lambda i, j: (i, j, 0))],
        out_shape=[jax.ShapeDtypeStruct((b, half, s), F32),
                   jax.ShapeDtypeStruct((b, half, s), F32),
                   jax.ShapeDtypeStruct((b, s, HEAD_PAD), F32),
                   jax.ShapeDtypeStruct((b, s, HEAD_PAD), F32)],
        compiler_params=pltpu.CompilerParams(dimension_semantics=("parallel", "parallel")),
        name="rope_tables",
    )(positions.reshape(b, 1, s), invf)


def _mem_kernel(mem_ref, g_ref, w_ref, kg_ref, mk_ref, mv_ref):
    m = mem_ref[0]
    mn = (_rms(m) * g_ref[0]).astype(BF16)
    mkv = jnp.dot(mn, w_ref[0], preferred_element_type=F32)
    for h in range(MEM_HEADS):
        kh = mkv[:, h * 2 * MEM_HD: h * 2 * MEM_HD + MEM_HD]
        vh = mkv[:, h * 2 * MEM_HD + MEM_HD: (h + 1) * 2 * MEM_HD]
        mk_ref[0, 0, h] = (_rms(kh) * kg_ref[0]).astype(BF16)
        mv_ref[0, 0, h] = vh.astype(BF16)


def _mem_kv(mem, mem_norm_g, w_mkv_bf, mem_k_g):
    depth = w_mkv_bf.shape[0]
    b, m, d = mem.shape
    return pl.pallas_call(
        _mem_kernel,
        grid=(depth, b),
        in_specs=[pl.BlockSpec((1, m, d), lambda l, i: (i, 0, 0)),
                  pl.BlockSpec((1, 1, d), lambda l, i: (l, 0, 0)),
                  pl.BlockSpec((1, d, 2 * MEM_W), lambda l, i: (l, 0, 0)),
                  pl.BlockSpec((1, 1, MEM_HD), lambda l, i: (l, 0, 0))],
        out_specs=[pl.BlockSpec((1, 1, MEM_HEADS, m, MEM_HD), lambda l, i: (l, i, 0, 0, 0)),
                   pl.BlockSpec((1, 1, MEM_HEADS, m, MEM_HD), lambda l, i: (l, i, 0, 0, 0))],
        out_shape=[jax.ShapeDtypeStruct((depth, b, MEM_HEADS, m, MEM_HD), BF16),
                   jax.ShapeDtypeStruct((depth, b, MEM_HEADS, m, MEM_HD), BF16)],
        compiler_params=pltpu.CompilerParams(dimension_semantics=("parallel", "parallel")),
        name="mem_kv",
    )(mem, mem_norm_g.reshape(depth, 1, d), w_mkv_bf, mem_k_g.reshape(depth, 1, MEM_HD))


def _pre_kernel(layer_ref, x_ref, ng_ref, w1_ref, qng_ref, kvng_ref, wuq_ref, wuk_ref, wuv_ref,
                qg_ref, qoff_ref, gn_ref, ga_ref, gb_ref, cosT_ref, sinT_ref, cosF_ref, sinF_ref,
                qT_ref, k_ref, vT_ref):
    del layer_ref
    x = x_ref[0]
    hb = (_rms(x) * ng_ref[0]).astype(BF16)
    a = jnp.dot(hb, w1_ref[0], preferred_element_type=F32)
    qn = (_rms(a[:, :Q_RANK]) * qng_ref[0]).astype(BF16)
    kvn = (_rms(a[:, OFF_KVLAT:OFF_KPE]) * kvng_ref[0]).astype(BF16)
    pe = a[:, OFF_KPE:OFF_KPE + HEAD_PAD]
    pe_sw = a[:, OFF_KPE + HEAD_PAD:W1_COLS]

    qT = lax.dot_general(wuq_ref[0], qn, NT_DIMS, preferred_element_type=F32)
    cT = cosT_ref[0]
    sT = sinT_ref[0]
    qg = qg_ref[0]
    qoff = qoff_ref[0]
    half = ROPE // 2
    for h in range(N_HEADS):
        qh = qT[h * HEAD_PAD:(h + 1) * HEAD_PAD]
        ms = jnp.sum(qh * qh, axis=0, keepdims=True) * (1.0 / QK_DIM)
        qh = qh * lax.rsqrt(ms + EPS) * qg + qoff
        t1 = qh[NOPE:NOPE + half]
        t2 = qh[NOPE + half:QK_DIM]
        qh = jnp.concatenate([qh[:NOPE], t1 * cT - t2 * sT, t2 * cT + t1 * sT, qh[QK_DIM:]], axis=0)
        qT_ref[0, h] = qh.astype(BF16)

    knope = jnp.dot(kvn, wuk_ref[0], preferred_element_type=F32)
    u = pe * (ga_ref[0] * cosF_ref[0]) + pe_sw * (gb_ref[0] * sinF_ref[0])
    ss_pe = jnp.sum(pe * pe, axis=-1, keepdims=True)
    gn = gn_ref[0]
    one_lane = (lax.broadcasted_iota(jnp.int32, (1, HEAD_PAD), 1) == QK_DIM).astype(F32)
    for h in range(N_HEADS):
        kn = knope[:, h * HEAD_PAD:(h + 1) * HEAD_PAD]
        ms = (jnp.sum(kn * kn, axis=-1, keepdims=True) + ss_pe) * (1.0 / QK_DIM)
        k_ref[0, h] = ((kn * gn + u) * lax.rsqrt(ms + EPS) + one_lane).astype(BF16)

    vT = lax.dot_general(wuv_ref[0], kvn, NT_DIMS, preferred_element_type=F32)
    vT_ref[0] = vT.reshape(N_HEADS, V_DIM, vT.shape[-1]).astype(BF16)


def _pre_call(layer, x, p, tabs):
    b, s, d = x.shape
    cosT, sinT, cosF, sinF = tabs
    half = ROPE // 2
    wspec = lambda shape: pl.BlockSpec((1,) + shape, lambda i, j, l: (l[0],) + (0,) * len(shape))
    grid_spec = pltpu.PrefetchScalarGridSpec(
        num_scalar_prefetch=1,
        grid=(b, s // TS_PRE),
        in_specs=[pl.BlockSpec((1, TS_PRE, d), lambda i, j, l: (i, j, 0)),
                  wspec((1, d)), wspec((d, W1_COLS)), wspec((1, Q_RANK)), wspec((1, KV_RANK)),
                  wspec((N_HEADS * HEAD_PAD, Q_RANK)), wspec((KV_RANK, N_HEADS * HEAD_PAD)),
                  wspec((N_HEADS * V_DIM, KV_RANK)),
                  wspec((HEAD_PAD, 1)), wspec((HEAD_PAD, 1)),
                  wspec((1, HEAD_PAD)), wspec((1, HEAD_PAD)), wspec((1, HEAD_PAD)),
                  pl.BlockSpec((1, half, TS_PRE), lambda i, j, l: (i, 0, j)),
                  pl.BlockSpec((1, half, TS_PRE), lambda i, j, l: (i, 0, j)),
                  pl.BlockSpec((1, TS_PRE, HEAD_PAD), lambda i, j, l: (i, j, 0)),
                  pl.BlockSpec((1, TS_PRE, HEAD_PAD), lambda i, j, l: (i, j, 0))],
        out_specs=[pl.BlockSpec((1, N_HEADS, HEAD_PAD, TS_PRE), lambda i, j, l: (i, 0, 0, j)),
                   pl.BlockSpec((1, N_HEADS, TS_PRE, HEAD_PAD), lambda i, j, l: (i, 0, j, 0)),
                   pl.BlockSpec((1, N_HEADS, V_DIM, TS_PRE), lambda i, j, l: (i, 0, 0, j))],
    )
    return pl.pallas_call(
        _pre_kernel,
        grid_spec=grid_spec,
        out_shape=[jax.ShapeDtypeStruct((b, N_HEADS, HEAD_PAD, s), BF16),
                   jax.ShapeDtypeStruct((b, N_HEADS, s, HEAD_PAD), BF16),
                   jax.ShapeDtypeStruct((b, N_HEADS, V_DIM, s), BF16)],
        compiler_params=pltpu.CompilerParams(dimension_semantics=("parallel", "parallel"),
                                             vmem_limit_bytes=VMEM_LIMIT),
        name="pre_attn",
    )(layer, x, p["norm_g"], p["w1"], p["q_norm_g"], p["kv_norm_g"], p["w_uqT"], p["w_uk"], p["w_uvT"],
      p["qg"], p["qoff"], p["gn"], p["ga"], p["gb"], cosT, sinT, cosF, sinF)


def _attn_kernel(qT_ref, k_ref, vT_ref, oT_ref, s_scr):
    qT = qT_ref[0, 0]
    tq = qT.shape[1]
    n_groups = k_ref.shape[2] // TK

    def scores(g, slot):
        start = pl.multiple_of(g * TK, TK)
        s = jnp.dot(k_ref[0, 0, pl.ds(start, TK), :], qT, preferred_element_type=F32)
        s_scr[slot] = s
        return jnp.max(s, axis=0, keepdims=True)

    def accumulate(g, slot, m, l, acc, m_grp):
        start = pl.multiple_of(g * TK, TK)
        m_new = jnp.maximum(m, m_grp)
        alpha = jnp.exp2(m - m_new)
        p = jnp.exp2(s_scr[slot] - m_new)
        l = l * alpha + jnp.sum(p, axis=0, keepdims=True)
        pv = jnp.dot(vT_ref[0, 0, :, pl.ds(start, TK)], p.astype(BF16), preferred_element_type=F32)
        return m_new, l, acc * alpha + pv

    def pair(i, carry):
        m, l, acc, m_grp = carry
        g = 2 * i
        m_grp1 = scores(g + 1, 1)
        m, l, acc = accumulate(g, 0, m, l, acc, m_grp)
        m_grp2 = scores(g + 2, 0)
        m, l, acc = accumulate(g + 1, 1, m, l, acc, m_grp1)
        return m, l, acc, m_grp2

    init = (jnp.full((1, tq), -1e30, F32), jnp.zeros((1, tq), F32), jnp.zeros((V_DIM, tq), F32), scores(0, 0))
    m, l, acc, m_grp = lax.fori_loop(0, n_groups // 2 - 1, pair, init)
    g = n_groups - 2
    m_grp1 = scores(g + 1, 1)
    m, l, acc = accumulate(g, 0, m, l, acc, m_grp)
    m, l, acc = accumulate(g + 1, 1, m, l, acc, m_grp1)
    oT_ref[0, 0] = acc / l


def _attn_bounded_kernel(qT_ref, k_ref, vT_ref, oT_ref):
    n_chunks = k_ref.shape[2] // TKB
    n_tiles = qT_ref.shape[3] // TQB

    def query_tile(j, carry):
        q0 = pl.multiple_of(j * TQB, TQB)
        qT = qT_ref[0, 0, :, pl.ds(q0, TQB)].astype(F32)

        def scores(c):
            return jnp.dot(k_ref[0, 0, c * TKB:(c + 1) * TKB, :].astype(F32), qT,
                           preferred_element_type=F32)

        l8 = jnp.zeros((SUBLANES, TQB), F32)
        acc = jnp.zeros((V_DIM, TQB), F32)
        s_next = scores(0)
        for c in range(n_chunks):
            s_cur = s_next
            if c + 1 < n_chunks:
                s_next = scores(c + 1)
            p = jnp.exp2(s_cur)
            l8 = l8 + jnp.sum(p.reshape(TKB // SUBLANES, SUBLANES, TQB), axis=0)
            acc = acc + jnp.dot(vT_ref[0, 0, :, c * TKB:(c + 1) * TKB], p.astype(BF16),
                                preferred_element_type=F32)
        oT_ref[0, 0, :, pl.ds(q0, TQB)] = acc / jnp.sum(l8, axis=0, keepdims=True)
        return carry

    lax.fori_loop(0, n_tiles, query_tile, 0)


def _attn_bounded_call(qT, k, vT):
    b, nh, _, s = qT.shape
    assert s % TKB == 0 and s % TQB == 0
    return pl.pallas_call(
        _attn_bounded_kernel,
        grid=(b, nh),
        in_specs=[pl.BlockSpec((1, 1, HEAD_PAD, s), lambda i, h: (i, h, 0, 0)),
                  pl.BlockSpec((1, 1, s, HEAD_PAD), lambda i, h: (i, h, 0, 0)),
                  pl.BlockSpec((1, 1, V_DIM, s), lambda i, h: (i, h, 0, 0))],
        out_specs=pl.BlockSpec((1, 1, V_DIM, s), lambda i, h: (i, h, 0, 0)),
        out_shape=jax.ShapeDtypeStruct((b, nh, V_DIM, s), F32),
        compiler_params=pltpu.CompilerParams(dimension_semantics=("parallel", "parallel"),
                                             vmem_limit_bytes=VMEM_LIMIT),
        name="mla_attn_bounded",
    )(qT, k, vT)


def _attn_call(qT, k, vT):
    b, nh, _, s = qT.shape
    assert (s // TK) % 2 == 0
    return pl.pallas_call(
        _attn_kernel,
        grid=(b, nh, s // TQ),
        in_specs=[pl.BlockSpec((1, 1, HEAD_PAD, TQ), lambda i, h, j: (i, h, 0, j)),
                  pl.BlockSpec((1, 1, s, HEAD_PAD), lambda i, h, j: (i, h, 0, 0)),
                  pl.BlockSpec((1, 1, V_DIM, s), lambda i, h, j: (i, h, 0, 0))],
        out_specs=pl.BlockSpec((1, 1, V_DIM, TQ), lambda i, h, j: (i, h, 0, j)),
        out_shape=jax.ShapeDtypeStruct((b, nh, V_DIM, s), F32),
        scratch_shapes=[pltpu.VMEM((2, TK, TQ), F32)],
        compiler_params=pltpu.CompilerParams(dimension_semantics=("parallel", "parallel", "arbitrary"),
                                             vmem_limit_bytes=VMEM_LIMIT),
        name="mla_attn",
    )(qT, k, vT)


def _post_kernel(layer_ref, x_ref, xp_ref, xn_ref, oT_ref, mk_ref, mv_ref, ng_ref, wr_ref, bg_ref,
                 cw_ref, cb_ref, mqg_ref, wa_ref, wc_ref, wm_ref, wo_ref, out_ref):
    del layer_ref
    j = pl.program_id(1)
    nj = pl.num_programs(1)
    x = x_ref[0]
    t = x.shape[0]
    ng = ng_ref[0]
    hb = (_rms(x) * ng).astype(BF16)
    xh = jnp.concatenate([xp_ref[0, 0], xn_ref[0, 0]], axis=0)
    hh = (_rms(xh) * ng).astype(BF16)
    h_ext = jnp.concatenate([hb, hh], axis=0)

    def proj(lhs, lo, hi):
        return jnp.dot(lhs, wr_ref[0, :, lo:hi], preferred_element_type=F32)

    ccu = proj(h_ext, R_CC, R_QM)
    z_ext = ccu[:, :CONV_W] * ccu[:, CONV_W:]
    z = z_ext[:t]
    z_before = z_ext[t + HALO - 1:t + HALO] * (j > 0).astype(F32)
    z_after = z_ext[t + HALO:t + HALO + 1] * (j < nj - 1).astype(F32)
    row = lax.broadcasted_iota(jnp.int32, z.shape, 0)
    z_prev = jnp.where(row == 0, z_before, pltpu.roll(z, 1, 0))
    z_next = jnp.where(row == t - 1, z_after, pltpu.roll(z, t - 1, 0))
    cw = cw_ref[0]
    conv = z_prev * cw[0:1] + z * cw[1:2] + z_next * cw[2:3] + cb_ref[0]
    o_conv = proj(hb, R_CB, R_CC) * conv * _silu(proj(hb, R_GC, R_GM))
    y_conv = jnp.dot(o_conv.astype(BF16), wc_ref[0], preferred_element_type=F32)

    oT = oT_ref[0]
    o_attn = oT.reshape(MLA_W, t).T * _silu(proj(hb, R_GA, R_GC))
    y_attn = jnp.dot(o_attn.astype(BF16), wa_ref[0], preferred_element_type=F32)

    qm = proj(hb, R_QM, R_GA)
    mqg = mqg_ref[0]
    heads = []
    for h in range(MEM_HEADS):
        qh = (_rms(qm[:, h * MEM_HD:(h + 1) * MEM_HD]) * mqg).astype(BF16)
        s = lax.dot_general(qh, mk_ref[0, 0, h], NT_DIMS, preferred_element_type=F32)
        s = s * (MEM_HD ** -0.5)
        p = jnp.exp(s - jnp.max(s, axis=-1, keepdims=True))
        l = jnp.sum(p, axis=-1, keepdims=True)
        oh = jnp.dot(p.astype(BF16), mv_ref[0, 0, h], preferred_element_type=F32)
        heads.append(oh / l)
    o_mem = jnp.concatenate(heads, axis=-1) * _silu(proj(hb, R_GM, R_R))
    y_mem = jnp.dot(o_mem.astype(BF16), wm_ref[0], preferred_element_type=F32)

    bg = bg_ref[0]
    r_a = _sigmoid(proj(hb, R_R, R_R + D_MODEL) + bg[:, :D_MODEL])
    y = r_a * y_attn
    r_c = _sigmoid(proj(hb, R_R + D_MODEL, R_R + 2 * D_MODEL) + bg[:, D_MODEL:2 * D_MODEL])
    y = y + r_c * y_conv
    r_m = _sigmoid(proj(hb, R_R + 2 * D_MODEL, R_R + 3 * D_MODEL) + bg[:, 2 * D_MODEL:])
    y = y + r_m * y_mem
    out_ref[0] = x + jnp.dot(y.astype(BF16), wo_ref[0], preferred_element_type=F32)


def _post_call(layer, x, oT, mk, mv, p):
    b, s, d = x.shape
    m = mk.shape[3]
    nblk = TS_POST // HALO
    last = s // HALO - 1
    x4 = x.reshape(b, s // HALO, HALO, d)
    wspec = lambda shape: pl.BlockSpec((1,) + shape, lambda i, j, l: (l[0],) + (0,) * len(shape),
                                       pipeline_mode=pl.Buffered(1))
    grid_spec = pltpu.PrefetchScalarGridSpec(
        num_scalar_prefetch=1,
        grid=(b, s // TS_POST),
        in_specs=[pl.BlockSpec((1, TS_POST, d), lambda i, j, l: (i, j, 0)),
                  pl.BlockSpec((1, 1, HALO, d), lambda i, j, l: (i, jnp.maximum(j * nblk - 1, 0), 0, 0)),
                  pl.BlockSpec((1, 1, HALO, d), lambda i, j, l: (i, jnp.minimum((j + 1) * nblk, last), 0, 0)),
                  pl.BlockSpec((1, N_HEADS, V_DIM, TS_POST), lambda i, j, l: (i, 0, 0, j)),
                  pl.BlockSpec((1, 1, MEM_HEADS, m, MEM_HD), lambda i, j, l: (l[0], i, 0, 0, 0)),
                  pl.BlockSpec((1, 1, MEM_HEADS, m, MEM_HD), lambda i, j, l: (l[0], i, 0, 0, 0)),
                  wspec((1, d)), wspec((d, REST_COLS)), wspec((1, N_BRANCH * d)),
                  wspec((3, CONV_W)), wspec((1, CONV_W)), wspec((1, MEM_HD)),
                  wspec((MLA_W, d)), wspec((CONV_W, d)), wspec((MEM_W, d)), wspec((d, d))],
        out_specs=pl.BlockSpec((1, TS_POST, d), lambda i, j, l: (i, j, 0)),
    )
    return pl.pallas_call(
        _post_kernel,
        grid_spec=grid_spec,
        out_shape=jax.ShapeDtypeStruct((b, s, d), F32),
        compiler_params=pltpu.CompilerParams(dimension_semantics=("parallel", "parallel"),
                                             vmem_limit_bytes=VMEM_LIMIT),
        name="post_attn",
    )(layer, x, x4, x4, oT, mk, mv, p["norm_g"], p["w_rest"], p["b_gate"], p["conv_w"], p["conv_b"],
      p["mem_q_g"], p["w_br_attn"], p["w_br_conv"], p["w_br_mem"], p["w_out"])


def _prepare_params(norm_g, w_in, b_gate, q_norm_g, w_uq, kv_norm_g, w_ukv, q_head_g, k_head_g,
                    conv_w, conv_b, mem_q_g, w_br_attn, w_br_conv, w_br_mem, w_out):
    depth = w_in.shape[0]
    d = D_MODEL
    half = ROPE // 2
    z = lambda n: jnp.zeros((depth, d, n), F32)
    t1 = w_in[:, :, OFF_KPE:OFF_KPE + half]
    t2 = w_in[:, :, OFF_KPE + half:OFF_REST]
    pad = HEAD_PAD - QK_DIM
    w1 = jnp.concatenate([w_in[:, :, :OFF_KPE], z(NOPE), t1, t2, z(pad), z(NOPE), t2, t1, z(pad)], axis=-1)

    w_uq4 = w_uq.reshape(depth, Q_RANK, N_HEADS, QK_DIM)
    w_uq_pad = jnp.pad(w_uq4, ((0, 0), (0, 0), (0, 0), (0, pad))).reshape(depth, Q_RANK, N_HEADS * HEAD_PAD)
    w_ukv4 = w_ukv.reshape(depth, KV_RANK, N_HEADS, NOPE + V_DIM)
    w_uk = jnp.pad(w_ukv4[..., :NOPE], ((0, 0), (0, 0), (0, 0), (0, HEAD_PAD - NOPE)))
    w_uv = w_ukv4[..., NOPE:].reshape(depth, KV_RANK, N_HEADS * V_DIM)

    zg = lambda n: jnp.zeros((depth, n), F32)
    q_scale = (QK_DIM ** -0.5) * LOG2E
    qg = jnp.concatenate([q_head_g * q_scale, zg(pad)], axis=-1)[..., None]
    score_bound = (QK_DIM ** 0.5) * LOG2E * jnp.max(jnp.abs(q_head_g), axis=-1) * jnp.max(jnp.abs(k_head_g), axis=-1)
    bounded = score_bound <= MAX_SCORE_BOUND
    off = jnp.where(bounded, -score_bound, 0.0)
    qoff = jnp.concatenate([zg(QK_DIM), jnp.broadcast_to(off[:, None], (depth, pad))], axis=-1)[..., None]
    g_nope, g1, g2 = k_head_g[:, :NOPE], k_head_g[:, NOPE:NOPE + half], k_head_g[:, NOPE + half:]
    gn = jnp.concatenate([g_nope, zg(HEAD_PAD - NOPE)], axis=-1)[:, None, :]
    ga = jnp.concatenate([zg(NOPE), g1, g2, zg(pad)], axis=-1)[:, None, :]
    gb = jnp.concatenate([zg(NOPE), g2, g1, zg(pad)], axis=-1)[:, None, :]
    return dict(
        norm_g=norm_g[:, None, :],
        w1=w1.astype(BF16),
        q_norm_g=q_norm_g[:, None, :],
        kv_norm_g=kv_norm_g[:, None, :],
        w_uqT=jnp.swapaxes(w_uq_pad, 1, 2).astype(BF16),
        w_uk=w_uk.reshape(depth, KV_RANK, N_HEADS * HEAD_PAD).astype(BF16),
        w_uvT=jnp.swapaxes(w_uv, 1, 2).astype(BF16),
        qg=qg, qoff=qoff, bounded=bounded, gn=gn, ga=ga, gb=gb,
        w_rest=w_in[:, :, OFF_REST:].astype(BF16),
        b_gate=b_gate[:, None, :],
        conv_w=conv_w,
        conv_b=conv_b[:, None, :],
        mem_q_g=mem_q_g[:, None, :],
        w_br_attn=w_br_attn.astype(BF16),
        w_br_conv=w_br_conv.astype(BF16),
        w_br_mem=w_br_mem.astype(BF16),
        w_out=w_out.astype(BF16),
    )


def kernel(x, mem, positions, norm_g, w_in, b_gate, q_norm_g, w_uq, kv_norm_g, w_ukv, q_head_g, k_head_g,
           conv_w, conv_b, mem_norm_g, w_mkv, mem_q_g, mem_k_g, w_br_attn, w_br_conv, w_br_mem, w_out):
    depth = w_in.shape[0]
    assert x.shape[-1] == D_MODEL and w_in.shape[-1] == OFF_REST + REST_COLS
    assert x.shape[1] % max(TS_PRE, TS_POST, TQ, TK, TS_TAB) == 0
    p = _prepare_params(norm_g, w_in, b_gate, q_norm_g, w_uq, kv_norm_g, w_ukv, q_head_g, k_head_g,
                        conv_w, conv_b, mem_q_g, w_br_attn, w_br_conv, w_br_mem, w_out)
    invf = ROPE_BASE ** (-jnp.arange(0, ROPE, 2, dtype=F32) / ROPE)
    tabs = _rope_tables(positions, invf[:, None])
    mk, mv = _mem_kv(mem, mem_norm_g, w_mkv.astype(BF16), mem_k_g)

    for i in range(depth):
        layer = jnp.full((1,), i, jnp.int32)
        qT, k, vT = _pre_call(layer, x, p, tabs)
        oT = lax.cond(p["bounded"][i], _attn_bounded_call, _attn_call, qT, k, vT)
        x = _post_call(layer, x, oT, mk, mv, p)
    return x
```

```python
import functools
import math

import jax
import jax.numpy as jnp
from jax import lax
from jax.experimental import pallas as pl
from jax.experimental.pallas import tpu as pltpu

D_MODEL = 1024
N_HEADS = 8
NOPE = 64
ROPE = 32
QK_DIM = NOPE + ROPE
V_DIM = 64
Q_RANK = 3 * D_MODEL // 8
KV_RANK = D_MODEL // 4
MLA_W = N_HEADS * V_DIM
CONV_W = D_MODEL // 2
MEM_HEADS = 4
MEM_HD = 128
MEM_W = MEM_HEADS * MEM_HD
N_BRANCH = 3
ROPE_BASE = 10000.0
EPS = 1e-6
LOG2E = math.log2(math.e)

LANES = 128
SUBLANES = 8
HEAD_PAD = LANES
VMEM_LIMIT = 56 * 1024 * 1024

OFF_KVLAT = Q_RANK
OFF_KPE = Q_RANK + KV_RANK
OFF_REST = OFF_KPE + ROPE
W1_COLS = OFF_KPE + 2 * HEAD_PAD
R_CB, R_CC, R_CU, R_QM, R_GA, R_GC, R_GM, R_R = (0, 512, 1024, 1536, 2048, 2560, 3072, 3584)
REST_COLS = R_R + N_BRANCH * D_MODEL

TS_TAB = 512
TS_PRE = 512
TS_POST = 512
TQ = 256
TK = 1024
HALO = SUBLANES
TQB = 1024
TKB = 256
MAX_SCORE_BOUND = 48.0

F32 = jnp.float32
BF16 = jnp.bfloat16
NT_DIMS = (((1,), (1,)), ((), ()))


def _rms(t, axis=-1):
    return t * lax.rsqrt(jnp.mean(t * t, axis=axis, keepdims=True) + EPS)


def _sigmoid(t):
    return 1.0 / (1.0 + jnp.exp(-t))


def _silu(t):
    return t * _sigmoid(t)


def _tables_kernel(pos_ref, invf_ref, cosT_ref, sinT_ref, cosF_ref, sinF_ref):
    pos = pos_ref[0].astype(F32)
    ang = invf_ref[...] * pos
    c = jnp.cos(ang)
    s = jnp.sin(ang)
    cosT_ref[0] = c
    sinT_ref[0] = s
    t = pos.shape[1]
    z64 = jnp.zeros((NOPE, t), F32)
    z32 = jnp.zeros((HEAD_PAD - QK_DIM, t), F32)
    cosF_ref[0] = jnp.concatenate([z64, c, c, z32], axis=0).T
    sinF_ref[0] = jnp.concatenate([z64, -s, s, z32], axis=0).T


def _rope_tables(positions, invf):
    b, s = positions.shape
    half = ROPE // 2
    return pl.pallas_call(
        _tables_kernel,
        grid=(b, s // TS_TAB),
        in_specs=[pl.BlockSpec((1, 1, TS_TAB), lambda i, j: (i, 0, j)),
                  pl.BlockSpec((half, 1), lambda i, j: (0, 0))],
        out_specs=[pl.BlockSpec((1, half, TS_TAB), lambda i, j: (i, 0, j)),
                   pl.BlockSpec((1, half, TS_TAB), lambda i, j: (i, 0, j)),
                   pl.BlockSpec((1, TS_TAB, HEAD_PAD), lambda i, j: (i, j, 0)),
                   pl.BlockSpec((1, TS_TAB, HEAD_PAD), lambda i, j: (i, j, 0))],
        out_shape=[jax.ShapeDtypeStruct((b, half, s), F32),
                   jax.ShapeDtypeStruct((b, half, s), F32),
                   jax.ShapeDtypeStruct((b, s, HEAD_PAD), F32),
                   jax.ShapeDtypeStruct((b, s, HEAD_PAD), F32)],
        compiler_params=pltpu.CompilerParams(dimension_semantics=("parallel", "parallel")),
        name="rope_tables",
    )(positions.reshape(b, 1, s), invf)


def _mem_kernel(mem_ref, g_ref, w_ref, kg_ref, mk_ref, mv_ref):
    m = mem_ref[0]
    mn = (_rms(m) * g_ref[0]).astype(BF16)
    mkv = jnp.dot(mn, w_ref[0], preferred_element_type=F32)
    for h in range(MEM_HEADS):
        kh = mkv[:, h * 2 * MEM_HD: h * 2 * MEM_HD + MEM_HD]
        vh = mkv[:, h * 2 * MEM_HD + MEM_HD: (h + 1) * 2 * MEM_HD]
        mk_ref[0, 0, h] = (_rms(kh) * kg_ref[0]).astype(BF16)
        mv_ref[0, 0, h] = vh.astype(BF16)


def _mem_kv(mem, mem_norm_g, w_mkv_bf, mem_k_g):
    depth = w_mkv_bf.shape[0]
    b, m, d = mem.shape
    return pl.pallas_call(
        _mem_kernel,
        grid=(depth, b),
        in_specs=[pl.BlockSpec((1, m, d), lambda l, i: (i, 0, 0)),
                  pl.BlockSpec((1, 1, d), lambda l, i: (l, 0, 0)),
                  pl.BlockSpec((1, d, 2 * MEM_W), lambda l, i: (l, 0, 0)),
                  pl.BlockSpec((1, 1, MEM_HD), lambda l, i: (l, 0, 0))],
        out_specs=[pl.BlockSpec((1, 1, MEM_HEADS, m, MEM_HD), lambda l, i: (l, i, 0, 0, 0)),
                   pl.BlockSpec((1, 1, MEM_HEADS, m, MEM_HD), lambda l, i: (l, i, 0, 0, 0))],
        out_shape=[jax.ShapeDtypeStruct((depth, b, MEM_HEADS, m, MEM_HD), BF16),
                   jax.ShapeDtypeStruct((depth, b, MEM_HEADS, m, MEM_HD), BF16)],
        compiler_params=pltpu.CompilerParams(dimension_semantics=("parallel", "parallel")),
        name="mem_kv",
    )(mem, mem_norm_g.reshape(depth, 1, d), w_mkv_bf, mem_k_g.reshape(depth, 1, MEM_HD))


def _pre_kernel(layer_ref, x_ref, ng_ref, w1_ref, qng_ref, kvng_ref, wuq_ref, wuk_ref, wuv_ref,
                qg_ref, qoff_ref, gn_ref, ga_ref, gb_ref, cosT_ref, sinT_ref, cosF_ref, sinF_ref,
                qT_ref, k_ref, vT_ref):
    del layer_ref
    x = x_ref[0]
    hb = (_rms(x) * ng_ref[0]).astype(BF16)
    a = jnp.dot(hb, w1_ref[0], preferred_element_type=F32)
    qn = (_rms(a[:, :Q_RANK]) * qng_ref[0]).astype(BF16)
    kvn = (_rms(a[:, OFF_KVLAT:OFF_KPE]) * kvng_ref[0]).astype(BF16)
    pe = a[:, OFF_KPE:OFF_KPE + HEAD_PAD]
    pe_sw = a[:, OFF_KPE + HEAD_PAD:W1_COLS]

    qT = lax.dot_general(wuq_ref[0], qn, NT_DIMS, preferred_element_type=F32)
    cT = cosT_ref[0]
    sT = sinT_ref[0]
    qg = qg_ref[0]
    qoff = qoff_ref[0]
    half = ROPE // 2
    for h in range(N_HEADS):
        qh = qT[h * HEAD_PAD:(h + 1) * HEAD_PAD]
        ms = jnp.sum(qh * qh, axis=0, keepdims=True) * (1.0 / QK_DIM)
        qh = qh * lax.rsqrt(ms + EPS) * qg + qoff
        t1 = qh[NOPE:NOPE + half]
        t2 = qh[NOPE + half:QK_DIM]
        qh = jnp.concatenate([qh[:NOPE], t1 * cT - t2 * sT, t2 * cT + t1 * sT, qh[QK_DIM:]], axis=0)
        qT_ref[0, h] = qh.astype(BF16)

    knope = jnp.dot(kvn, wuk_ref[0], preferred_element_type=F32)
    u = pe * (ga_ref[0] * cosF_ref[0]) + pe_sw * (gb_ref[0] * sinF_ref[0])
    ss_pe = jnp.sum(pe * pe, axis=-1, keepdims=True)
    gn = gn_ref[0]
    one_lane = (lax.broadcasted_iota(jnp.int32, (1, HEAD_PAD), 1) == QK_DIM).astype(F32)
    for h in range(N_HEADS):
        kn = knope[:, h * HEAD_PAD:(h + 1) * HEAD_PAD]
        ms = (jnp.sum(kn * kn, axis=-1, keepdims=True) + ss_pe) * (1.0 / QK_DIM)
        k_ref[0, h] = ((kn * gn + u) * lax.rsqrt(ms + EPS) + one_lane).astype(BF16)

    vT = lax.dot_general(wuv_ref[0], kvn, NT_DIMS, preferred_element_type=F32)
    vT_ref[0] = vT.reshape(N_HEADS, V_DIM, vT.shape[-1]).astype(BF16)


def _pre_call(layer, x, p, tabs):
    b, s, d = x.shape
    cosT, sinT, cosF, sinF = tabs
    half = ROPE // 2
    wspec = lambda shape: pl.BlockSpec((1,) + shape, lambda i, j, l: (l[0],) + (0,) * len(shape))
    grid_spec = pltpu.PrefetchScalarGridSpec(
        num_scalar_prefetch=1,
        grid=(b, s // TS_PRE),
        in_specs=[pl.BlockSpec((1, TS_PRE, d), lambda i, j, l: (i, j, 0)),
                  wspec((1, d)), wspec((d, W1_COLS)), wspec((1, Q_RANK)), wspec((1, KV_RANK)),
                  wspec((N_HEADS * HEAD_PAD, Q_RANK)), wspec((KV_RANK, N_HEADS * HEAD_PAD)),
                  wspec((N_HEADS * V_DIM, KV_RANK)),
                  wspec((HEAD_PAD, 1)), wspec((HEAD_PAD, 1)),
                  wspec((1, HEAD_PAD)), wspec((1, HEAD_PAD)), wspec((1, HEAD_PAD)),
                  pl.BlockSpec((1, half, TS_PRE), lambda i, j, l: (i, 0, j)),
                  pl.BlockSpec((1, half, TS_PRE), lambda i, j, l: (i, 0, j)),
                  pl.BlockSpec((1, TS_PRE, HEAD_PAD), lambda i, j, l: (i, j, 0)),
                  pl.BlockSpec((1, TS_PRE, HEAD_PAD), lambda i, j, l: (i, j, 0))],
        out_specs=[pl.BlockSpec((1, N_HEADS, HEAD_PAD, TS_PRE), lambda i, j, l: (i, 0, 0, j)),
                   pl.BlockSpec((1, N_HEADS, TS_PRE, HEAD_PAD), lambda i, j, l: (i, 0, j, 0)),
                   pl.BlockSpec((1, N_HEADS, V_DIM, TS_PRE), lambda i, j, l: (i, 0, 0, j))],
    )
    return pl.pallas_call(
        _pre_kernel,
        grid_spec=grid_spec,
        out_shape=[jax.ShapeDtypeStruct((b, N_HEADS, HEAD_PAD, s), BF16),
                   jax.ShapeDtypeStruct((b, N_HEADS, s, HEAD_PAD), BF16),
                   jax.ShapeDtypeStruct((b, N_HEADS, V_DIM, s), BF16)],
        compiler_params=pltpu.CompilerParams(dimension_semantics=("parallel", "parallel"),
                                             vmem_limit_bytes=VMEM_LIMIT),
        name="pre_attn",
    )(layer, x, p["norm_g"], p["w1"], p["q_norm_g"], p["kv_norm_g"], p["w_uqT"], p["w_uk"], p["w_uvT"],
      p["qg"], p["qoff"], p["gn"], p["ga"], p["gb"], cosT, sinT, cosF, sinF)


def _attn_kernel(qT_ref, k_ref, vT_ref, oT_ref, s_scr):
    qT = qT_ref[0, 0]
    tq = qT.shape[1]
    n_groups = k_ref.shape[2] // TK

    def scores(g, slot):
        start = pl.multiple_of(g * TK, TK)
        s = jnp.dot(k_ref[0, 0, pl.ds(start, TK), :], qT, preferred_element_type=F32)
        s_scr[slot] = s
        return jnp.max(s, axis=0, keepdims=True)

    def accumulate(g, slot, m, l, acc, m_grp):
        start = pl.multiple_of(g * TK, TK)
        m_new = jnp.maximum(m, m_grp)
        alpha = jnp.exp2(m - m_new)
        p = jnp.exp2(s_scr[slot] - m_new)
        l = l * alpha + jnp.sum(p, axis=0, keepdims=True)
        pv = jnp.dot(vT_ref[0, 0, :, pl.ds(start, TK)], p.astype(BF16), preferred_element_type=F32)
        return m_new, l, acc * alpha + pv

    def pair(i, carry):
        m, l, acc, m_grp = carry
        g = 2 * i
        m_grp1 = scores(g + 1, 1)
        m, l, acc = accumulate(g, 0, m, l, acc, m_grp)
        m_grp2 = scores(g + 2, 0)
        m, l, acc = accumulate(g + 1, 1, m, l, acc, m_grp1)
        return m, l, acc, m_grp2

    init = (jnp.full((1, tq), -1e30, F32), jnp.zeros((1, tq), F32), jnp.zeros((V_DIM, tq), F32), scores(0, 0))
    m, l, acc, m_grp = lax.fori_loop(0, n_groups // 2 - 1, pair, init)
    g = n_groups - 2
    m_grp1 = scores(g + 1, 1)
    m, l, acc = accumulate(g, 0, m, l, acc, m_grp)
    m, l, acc = accumulate(g + 1, 1, m, l, acc, m_grp1)
    oT_ref[0, 0] = acc / l


def _attn_bounded_kernel(qT_ref, k_ref, vT_ref, oT_ref):
    n_chunks = k_ref.shape[2] // TKB
    n_tiles = qT_ref.shape[3] // TQB

    def query_tile(j, carry):
        q0 = pl.multiple_of(j * TQB, TQB)
        qT = qT_ref[0, 0, :, pl.ds(q0, TQB)]

        def scores(c):
            return jnp.dot(k_ref[0, 0, c * TKB:(c + 1) * TKB, :], qT, preferred_element_type=F32)

        l8 = jnp.zeros((SUBLANES, TQB), F32)
        acc = jnp.zeros((V_DIM, TQB), F32)
        s_next = scores(0)
        for c in range(n_chunks):
            s_cur = s_next
            if c + 1 < n_chunks:
                s_next = scores(c + 1)
            p = jnp.exp2(s_cur)
            l8 = l8 + jnp.sum(p.reshape(TKB // SUBLANES, SUBLANES, TQB), axis=0)
            acc = acc + jnp.dot(vT_ref[0, 0, :, c * TKB:(c + 1) * TKB], p.astype(BF16),
                                preferred_element_type=F32)
        oT_ref[0, 0, :, pl.ds(q0, TQB)] = acc / jnp.sum(l8, axis=0, keepdims=True)
        return carry

    lax.fori_loop(0, n_tiles, query_tile, 0)


def _attn_bounded_call(qT, k, vT):
    b, nh, _, s = qT.shape
    assert s % TKB == 0 and s % TQB == 0
    return pl.pallas_call(
        _attn_bounded_kernel,
        grid=(b, nh),
        in_specs=[pl.BlockSpec((1, 1, HEAD_PAD, s), lambda i, h: (i, h, 0, 0)),
                  pl.BlockSpec((1, 1, s, HEAD_PAD), lambda i, h: (i, h, 0, 0)),
                  pl.BlockSpec((1, 1, V_DIM, s), lambda i, h: (i, h, 0, 0))],
        out_specs=pl.BlockSpec((1, 1, V_DIM, s), lambda i, h: (i, h, 0, 0)),
        out_shape=jax.ShapeDtypeStruct((b, nh, V_DIM, s), F32),
        compiler_params=pltpu.CompilerParams(dimension_semantics=("parallel", "parallel"),
                                             vmem_limit_bytes=VMEM_LIMIT),
        name="mla_attn_bounded",
    )(qT, k, vT)


def _attn_call(qT, k, vT):
    b, nh, _, s = qT.shape
    assert (s // TK) % 2 == 0
    return pl.pallas_call(
        _attn_kernel,
        grid=(b, nh, s // TQ),
        in_specs=[pl.BlockSpec((1, 1, HEAD_PAD, TQ), lambda i, h, j: (i, h, 0, j)),
                  pl.BlockSpec((1, 1, s, HEAD_PAD), lambda i, h, j: (i, h, 0, 0)),
                  pl.BlockSpec((1, 1, V_DIM, s), lambda i, h, j: (i, h, 0, 0))],
        out_specs=pl.BlockSpec((1, 1, V_DIM, TQ), lambda i, h, j: (i, h, 0, j)),
        out_shape=jax.ShapeDtypeStruct((b, nh, V_DIM, s), F32),
        scratch_shapes=[pltpu.VMEM((2, TK, TQ), F32)],
        compiler_params=pltpu.CompilerParams(dimension_semantics=("parallel", "parallel", "arbitrary"),
                                             vmem_limit_bytes=VMEM_LIMIT),
        name="mla_attn",
    )(qT, k, vT)


def _post_kernel(layer_ref, x_ref, xp_ref, xn_ref, oT_ref, mk_ref, mv_ref, ng_ref, wr_ref, bg_ref,
                 cw_ref, cb_ref, mqg_ref, wa_ref, wc_ref, wm_ref, wo_ref, out_ref):
    del layer_ref
    j = pl.program_id(1)
    nj = pl.num_programs(1)
    x = x_ref[0]
    t = x.shape[0]
    ng = ng_ref[0]
    hb = (_rms(x) * ng).astype(BF16)
    xh = jnp.concatenate([xp_ref[0, 0], xn_ref[0, 0]], axis=0)
    hh = (_rms(xh) * ng).astype(BF16)
    h_ext = jnp.concatenate([hb, hh], axis=0)

    def proj(lhs, lo, hi):
        return jnp.dot(lhs, wr_ref[0, :, lo:hi], preferred_element_type=F32)

    ccu = proj(h_ext, R_CC, R_QM)
    z_ext = ccu[:, :CONV_W] * ccu[:, CONV_W:]
    z = z_ext[:t]
    z_before = z_ext[t + HALO - 1:t + HALO] * (j > 0).astype(F32)
    z_after = z_ext[t + HALO:t + HALO + 1] * (j < nj - 1).astype(F32)
    row = lax.broadcasted_iota(jnp.int32, z.shape, 0)
    z_prev = jnp.where(row == 0, z_before, pltpu.roll(z, 1, 0))
    z_next = jnp.where(row == t - 1, z_after, pltpu.roll(z, t - 1, 0))
    cw = cw_ref[0]
    conv = z_prev * cw[0:1] + z * cw[1:2] + z_next * cw[2:3] + cb_ref[0]
    o_conv = proj(hb, R_CB, R_CC) * conv * _silu(proj(hb, R_GC, R_GM))
    y_conv = jnp.dot(o_conv.astype(BF16), wc_ref[0], preferred_element_type=F32)

    oT = oT_ref[0]
    o_attn = oT.reshape(MLA_W, t).T * _silu(proj(hb, R_GA, R_GC))
    y_attn = jnp.dot(o_attn.astype(BF16), wa_ref[0], preferred_element_type=F32)

    qm = proj(hb, R_QM, R_GA)
    mqg = mqg_ref[0]
    heads = []
    for h in range(MEM_HEADS):
        qh = (_rms(qm[:, h * MEM_HD:(h + 1) * MEM_HD]) * mqg).astype(BF16)
        s = lax.dot_general(qh, mk_ref[0, 0, h], NT_DIMS, preferred_element_type=F32)
        s = s * (MEM_HD ** -0.5)
        p = jnp.exp(s - jnp.max(s, axis=-1, keepdims=True))
        l = jnp.sum(p, axis=-1, keepdims=True)
        oh = jnp.dot(p.astype(BF16), mv_ref[0, 0, h], preferred_element_type=F32)
        heads.append(oh / l)
    o_mem = jnp.concatenate(heads, axis=-1) * _silu(proj(hb, R_GM, R_R))
    y_mem = jnp.dot(o_mem.astype(BF16), wm_ref[0], preferred_element_type=F32)

    bg = bg_ref[0]
    r_a = _sigmoid(proj(hb, R_R, R_R + D_MODEL) + bg[:, :D_MODEL])
    y = r_a * y_attn
    r_c = _sigmoid(proj(hb, R_R + D_MODEL, R_R + 2 * D_MODEL) + bg[:, D_MODEL:2 * D_MODEL])
    y = y + r_c * y_conv
    r_m = _sigmoid(proj(hb, R_R + 2 * D_MODEL, R_R + 3 * D_MODEL) + bg[:, 2 * D_MODEL:])
    y = y + r_m * y_mem
    out_ref[0] = x + jnp.dot(y.astype(BF16), wo_ref[0], preferred_element_type=F32)


def _post_call(layer, x, oT, mk, mv, p):
    b, s, d = x.shape
    m = mk.shape[3]
    nblk = TS_POST // HALO
    last = s // HALO - 1
    x4 = x.reshape(b, s // HALO, HALO, d)
    wspec = lambda shape: pl.BlockSpec((1,) + shape, lambda i, j, l: (l[0],) + (0,) * len(shape),
                                       pipeline_mode=pl.Buffered(1))
    grid_spec = pltpu.PrefetchScalarGridSpec(
        num_scalar_prefetch=1,
        grid=(b, s // TS_POST),
        in_specs=[pl.BlockSpec((1, TS_POST, d), lambda i, j, l: (i, j, 0)),
                  pl.BlockSpec((1, 1, HALO, d), lambda i, j, l: (i, jnp.maximum(j * nblk - 1, 0), 0, 0)),
                  pl.BlockSpec((1, 1, HALO, d), lambda i, j, l: (i, jnp.minimum((j + 1) * nblk, last), 0, 0)),
                  pl.BlockSpec((1, N_HEADS, V_DIM, TS_POST), lambda i, j, l: (i, 0, 0, j)),
                  pl.BlockSpec((1, 1, MEM_HEADS, m, MEM_HD), lambda i, j, l: (l[0], i, 0, 0, 0)),
                  pl.BlockSpec((1, 1, MEM_HEADS, m, MEM_HD), lambda i, j, l: (l[0], i, 0, 0, 0)),
                  wspec((1, d)), wspec((d, REST_COLS)), wspec((1, N_BRANCH * d)),
                  wspec((3, CONV_W)), wspec((1, CONV_W)), wspec((1, MEM_HD)),
                  wspec((MLA_W, d)), wspec((CONV_W, d)), wspec((MEM_W, d)), wspec((d, d))],
        out_specs=pl.BlockSpec((1, TS_POST, d), lambda i, j, l: (i, j, 0)),
    )
    return pl.pallas_call(
        _post_kernel,
        grid_spec=grid_spec,
        out_shape=jax.ShapeDtypeStruct((b, s, d), F32),
        compiler_params=pltpu.CompilerParams(dimension_semantics=("parallel", "parallel"),
                                             vmem_limit_bytes=VMEM_LIMIT),
        name="post_attn",
    )(layer, x, x4, x4, oT, mk, mv, p["norm_g"], p["w_rest"], p["b_gate"], p["conv_w"], p["conv_b"],
      p["mem_q_g"], p["w_br_attn"], p["w_br_conv"], p["w_br_mem"], p["w_out"])


def _prepare_params(norm_g, w_in, b_gate, q_norm_g, w_uq, kv_norm_g, w_ukv, q_head_g, k_head_g,
                    conv_w, conv_b, mem_q_g, w_br_attn, w_br_conv, w_br_mem, w_out):
    depth = w_in.shape[0]
    d = D_MODEL
    half = ROPE // 2
    z = lambda n: jnp.zeros((depth, d, n), F32)
    t1 = w_in[:, :, OFF_KPE:OFF_KPE + half]
    t2 = w_in[:, :, OFF_KPE + half:OFF_REST]
    pad = HEAD_PAD - QK_DIM
    w1 = jnp.concatenate([w_in[:, :, :OFF_KPE], z(NOPE), t1, t2, z(pad), z(NOPE), t2, t1, z(pad)], axis=-1)

    w_uq4 = w_uq.reshape(depth, Q_RANK, N_HEADS, QK_DIM)
    w_uq_pad = jnp.pad(w_uq4, ((0, 0), (0, 0), (0, 0), (0, pad))).reshape(depth, Q_RANK, N_HEADS * HEAD_PAD)
    w_ukv4 = w_ukv.reshape(depth, KV_RANK, N_HEADS, NOPE + V_DIM)
    w_uk = jnp.pad(w_ukv4[..., :NOPE], ((0, 0), (0, 0), (0, 0), (0, HEAD_PAD - NOPE)))
    w_uv = w_ukv4[..., NOPE:].reshape(depth, KV_RANK, N_HEADS * V_DIM)

    zg = lambda n: jnp.zeros((depth, n), F32)
    q_scale = (QK_DIM ** -0.5) * LOG2E
    qg = jnp.concatenate([q_head_g * q_scale, zg(pad)], axis=-1)[..., None]
    score_bound = (QK_DIM ** 0.5) * LOG2E * jnp.max(jnp.abs(q_head_g), axis=-1) * jnp.max(jnp.abs(k_head_g), axis=-1)
    bounded = score_bound <= MAX_SCORE_BOUND
    off = jnp.where(bounded, -score_bound, 0.0)
    qoff = jnp.concatenate([zg(QK_DIM), jnp.broadcast_to(off[:, None], (depth, pad))], axis=-1)[..., None]
    g_nope, g1, g2 = k_head_g[:, :NOPE], k_head_g[:, NOPE:NOPE + half], k_head_g[:, NOPE + half:]
    gn = jnp.concatenate([g_nope, zg(HEAD_PAD - NOPE)], axis=-1)[:, None, :]
    ga = jnp.concatenate([zg(NOPE), g1, g2, zg(pad)], axis=-1)[:, None, :]
    gb = jnp.concatenate([zg(NOPE), g2, g1, zg(pad)], axis=-1)[:, None, :]
    return dict(
        norm_g=norm_g[:, None, :],
        w1=w1.astype(BF16),
        q_norm_g=q_norm_g[:, None, :],
        kv_norm_g=kv_norm_g[:, None, :],
        w_uqT=jnp.swapaxes(w_uq_pad, 1, 2).astype(BF16),
        w_uk=w_uk.reshape(depth, KV_RANK, N_HEADS * HEAD_PAD).astype(BF16),
        w_uvT=jnp.swapaxes(w_uv, 1, 2).astype(BF16),
        qg=qg, qoff=qoff, bounded=bounded, gn=gn, ga=ga, gb=gb,
        w_rest=w_in[:, :, OFF_REST:].astype(BF16),
        b_gate=b_gate[:, None, :],
        conv_w=conv_w,
        conv_b=conv_b[:, None, :],
        mem_q_g=mem_q_g[:, None, :],
        w_br_attn=w_br_attn.astype(BF16),
        w_br_conv=w_br_conv.astype(BF16),
        w_br_mem=w_br_mem.astype(BF16),
        w_out=w_out.astype(BF16),
    )


def kernel(x, mem, positions, norm_g, w_in, b_gate, q_norm_g, w_uq, kv_norm_g, w_ukv, q_head_g, k_head_g,
           conv_w, conv_b, mem_norm_g, w_mkv, mem_q_g, mem_k_g, w_br_attn, w_br_conv, w_br_mem, w_out):
    depth = w_in.shape[0]
    assert x.shape[-1] == D_MODEL and w_in.shape[-1] == OFF_REST + REST_COLS
    assert x.shape[1] % max(TS_PRE, TS_POST, TQ, TK, TS_TAB) == 0
    p = _prepare_params(norm_g, w_in, b_gate, q_norm_g, w_uq, kv_norm_g, w_ukv, q_head_g, k_head_g,
                        conv_w, conv_b, mem_q_g, w_br_attn, w_br_conv, w_br_mem, w_out)
    invf = ROPE_BASE ** (-jnp.arange(0, ROPE, 2, dtype=F32) / ROPE)
    tabs = _rope_tables(positions, invf[:, None])
    mk, mv = _mem_kv(mem, mem_norm_g, w_mkv.astype(BF16), mem_k_g)

    for i in range(depth):
        layer = jnp.full((1,), i, jnp.int32)
        qT, k, vT = _pre_call(layer, x, p, tabs)
        oT = lax.cond(p["bounded"][i], _attn_bounded_call, _attn_call, qT, k, vT)
        x = _post_call(layer, x, oT, mk, mv, p)
    return x
```

```python
import functools
import math

import jax
import jax.numpy as jnp
from jax import lax
from jax.experimental import pallas as pl
from jax.experimental.pallas import tpu as pltpu

D_MODEL = 1024
N_HEADS = 8
NOPE = 64
ROPE = 32
QK_DIM = NOPE + ROPE
V_DIM = 64
Q_RANK = 3 * D_MODEL // 8
KV_RANK = D_MODEL // 4
MLA_W = N_HEADS * V_DIM
CONV_W = D_MODEL // 2
MEM_HEADS = 4
MEM_HD = 128
MEM_W = MEM_HEADS * MEM_HD
N_BRANCH = 3
ROPE_BASE = 10000.0
EPS = 1e-6
LOG2E = math.log2(math.e)

LANES = 128
SUBLANES = 8
HEAD_PAD = LANES
VMEM_LIMIT = 56 * 1024 * 1024

OFF_KVLAT = Q_RANK
OFF_KPE = Q_RANK + KV_RANK
OFF_REST = OFF_KPE + ROPE
W1_COLS = OFF_KPE + 2 * HEAD_PAD
R_CB, R_CC, R_CU, R_QM, R_GA, R_GC, R_GM, R_R = (0, 512, 1024, 1536, 2048, 2560, 3072, 3584)
REST_COLS = R_R + N_BRANCH * D_MODEL

TS_TAB = 512
TS_PRE = 512
TS_POST = 512
TQ = 256
TK = 1024
HALO = SUBLANES
TQB = 1024
TQH = 512
TKB = 256
MAX_SCORE_BOUND = 48.0

F32 = jnp.float32
BF16 = jnp.bfloat16
NT_DIMS = (((1,), (1,)), ((), ()))


def _rms(t, axis=-1):
    return t * lax.rsqrt(jnp.mean(t * t, axis=axis, keepdims=True) + EPS)


def _sigmoid(t):
    return 1.0 / (1.0 + jnp.exp(-t))


def _silu(t):
    return t * _sigmoid(t)


def _tables_kernel(pos_ref, invf_ref, cosT_ref, sinT_ref, cosF_ref, sinF_ref):
    pos = pos_ref[0].astype(F32)
    ang = invf_ref[...] * pos
    c = jnp.cos(ang)
    s = jnp.sin(ang)
    cosT_ref[0] = c
    sinT_ref[0] = s
    t = pos.shape[1]
    z64 = jnp.zeros((NOPE, t), F32)
    z32 = jnp.zeros((HEAD_PAD - QK_DIM, t), F32)
    cosF_ref[0] = jnp.concatenate([z64, c, c, z32], axis=0).T
    sinF_ref[0] = jnp.concatenate([z64, -s, s, z32], axis=0).T


def _rope_tables(positions, invf):
    b, s = positions.shape
    half = ROPE // 2
    return pl.pallas_call(
        _tables_kernel,
        grid=(b, s // TS_TAB),
        in_specs=[pl.BlockSpec((1, 1, TS_TAB), lambda i, j: (i, 0, j)),
                  pl.BlockSpec((half, 1), lambda i, j: (0, 0))],
        out_specs=[pl.BlockSpec((1, half, TS_TAB), lambda i, j: (i, 0, j)),
                   pl.BlockSpec((1, half, TS_TAB), lambda i, j: (i, 0, j)),
                   pl.BlockSpec((1, TS_TAB, HEAD_PAD), lambda i, j: (i, j, 0)),
                   pl.BlockSpec((1, TS_TAB, HEAD_PAD), lambda i, j: (i, j, 0))],
        out_shape=[jax.ShapeDtypeStruct((b, half, s), F32),
                   jax.ShapeDtypeStruct((b, half, s), F32),
                   jax.ShapeDtypeStruct((b, s, HEAD_PAD), F32),
                   jax.ShapeDtypeStruct((b, s, HEAD_PAD), F32)],
        compiler_params=pltpu.CompilerParams(dimension_semantics=("parallel", "parallel")),
        name="rope_tables",
    )(positions.reshape(b, 1, s), invf)


def _mem_kernel(mem_ref, g_ref, w_ref, kg_ref, mk_ref, mv_ref):
    m = mem_ref[0]
    mn = (_rms(m) * g_ref[0]).astype(BF16)
    mkv = jnp.dot(mn, w_ref[0], preferred_element_type=F32)
    for h in range(MEM_HEADS):
        kh = mkv[:, h * 2 * MEM_HD: h * 2 * MEM_HD + MEM_HD]
        vh = mkv[:, h * 2 * MEM_HD + MEM_HD: (h + 1) * 2 * MEM_HD]
        mk_ref[0, 0, h] = (_rms(kh) * kg_ref[0]).astype(BF16)
        mv_ref[0, 0, h] = vh.astype(BF16)


def _mem_kv(mem, mem_norm_g, w_mkv_bf, mem_k_g):
    depth = w_mkv_bf.shape[0]
    b, m, d = mem.shape
    return pl.pallas_call(
        _mem_kernel,
        grid=(depth, b),
        in_specs=[pl.BlockSpec((1, m, d), lambda l, i: (i, 0, 0)),
                  pl.BlockSpec((1, 1, d), lambda l, i: (l, 0, 0)),
                  pl.BlockSpec((1, d, 2 * MEM_W), lambda l, i: (l, 0, 0)),
                  pl.BlockSpec((1, 1, MEM_HD), lambda l, i: (l, 0, 0))],
        out_specs=[pl.BlockSpec((1, 1, MEM_HEADS, m, MEM_HD), lambda l, i: (l, i, 0, 0, 0)),
                   pl.BlockSpec((1, 1, MEM_HEADS, m, MEM_HD), lambda l, i: (l, i, 0, 0, 0))],
        out_shape=[jax.ShapeDtypeStruct((depth, b, MEM_HEADS, m, MEM_HD), BF16),
                   jax.ShapeDtypeStruct((depth, b, MEM_HEADS, m, MEM_HD), BF16)],
        compiler_params=pltpu.CompilerParams(dimension_semantics=("parallel", "parallel")),
        name="mem_kv",
    )(mem, mem_norm_g.reshape(depth, 1, d), w_mkv_bf, mem_k_g.reshape(depth, 1, MEM_HD))


def _pre_kernel(layer_ref, x_ref, ng_ref, w1_ref, qng_ref, kvng_ref, wuq_ref, wuk_ref, wuv_ref,
                qg_ref, qoff_ref, gn_ref, ga_ref, gb_ref, cosT_ref, sinT_ref, cosF_ref, sinF_ref,
                qT_ref, k_ref, vT_ref):
    del layer_ref
    x = x_ref[0]
    hb = (_rms(x) * ng_ref[0]).astype(BF16)
    a = jnp.dot(hb, w1_ref[0], preferred_element_type=F32)
    qn = (_rms(a[:, :Q_RANK]) * qng_ref[0]).astype(BF16)
    kvn = (_rms(a[:, OFF_KVLAT:OFF_KPE]) * kvng_ref[0]).astype(BF16)
    pe = a[:, OFF_KPE:OFF_KPE + HEAD_PAD]
    pe_sw = a[:, OFF_KPE + HEAD_PAD:W1_COLS]

    qT = lax.dot_general(wuq_ref[0], qn, NT_DIMS, preferred_element_type=F32)
    cT = cosT_ref[0]
    sT = sinT_ref[0]
    qg = qg_ref[0]
    qoff = qoff_ref[0]
    half = ROPE // 2
    for h in range(N_HEADS):
        qh = qT[h * HEAD_PAD:(h + 1) * HEAD_PAD]
        ms = jnp.sum(qh * qh, axis=0, keepdims=True) * (1.0 / QK_DIM)
        qh = qh * lax.rsqrt(ms + EPS) * qg + qoff
        t1 = qh[NOPE:NOPE + half]
        t2 = qh[NOPE + half:QK_DIM]
        qh = jnp.concatenate([qh[:NOPE], t1 * cT - t2 * sT, t2 * cT + t1 * sT, qh[QK_DIM:]], axis=0)
        qT_ref[0, h] = qh.astype(BF16)

    knope = jnp.dot(kvn, wuk_ref[0], preferred_element_type=F32)
    u = pe * (ga_ref[0] * cosF_ref[0]) + pe_sw * (gb_ref[0] * sinF_ref[0])
    ss_pe = jnp.sum(pe * pe, axis=-1, keepdims=True)
    gn = gn_ref[0]
    one_lane = (lax.broadcasted_iota(jnp.int32, (1, HEAD_PAD), 1) == QK_DIM).astype(F32)
    for h in range(N_HEADS):
        kn = knope[:, h * HEAD_PAD:(h + 1) * HEAD_PAD]
        ms = (jnp.sum(kn * kn, axis=-1, keepdims=True) + ss_pe) * (1.0 / QK_DIM)
        k_ref[0, h] = ((kn * gn + u) * lax.rsqrt(ms + EPS) + one_lane).astype(BF16)

    vT = lax.dot_general(wuv_ref[0], kvn, NT_DIMS, preferred_element_type=F32)
    vT_ref[0] = vT.reshape(N_HEADS, V_DIM, vT.shape[-1]).astype(BF16)


def _pre_call(layer, x, p, tabs):
    b, s, d = x.shape
    cosT, sinT, cosF, sinF = tabs
    half = ROPE // 2
    wspec = lambda shape: pl.BlockSpec((1,) + shape, lambda i, j, l: (l[0],) + (0,) * len(shape))
    grid_spec = pltpu.PrefetchScalarGridSpec(
        num_scalar_prefetch=1,
        grid=(b, s // TS_PRE),
        in_specs=[pl.BlockSpec((1, TS_PRE, d), lambda i, j, l: (i, j, 0)),
                  wspec((1, d)), wspec((d, W1_COLS)), wspec((1, Q_RANK)), wspec((1, KV_RANK)),
                  wspec((N_HEADS * HEAD_PAD, Q_RANK)), wspec((KV_RANK, N_HEADS * HEAD_PAD)),
                  wspec((N_HEADS * V_DIM, KV_RANK)),
                  wspec((HEAD_PAD, 1)), wspec((HEAD_PAD, 1)),
                  wspec((1, HEAD_PAD)), wspec((1, HEAD_PAD)), wspec((1, HEAD_PAD)),
                  pl.BlockSpec((1, half, TS_PRE), lambda i, j, l: (i, 0, j)),
                  pl.BlockSpec((1, half, TS_PRE), lambda i, j, l: (i, 0, j)),
                  pl.BlockSpec((1, TS_PRE, HEAD_PAD), lambda i, j, l: (i, j, 0)),
                  pl.BlockSpec((1, TS_PRE, HEAD_PAD), lambda i, j, l: (i, j, 0))],
        out_specs=[pl.BlockSpec((1, N_HEADS, HEAD_PAD, TS_PRE), lambda i, j, l: (i, 0, 0, j)),
                   pl.BlockSpec((1, N_HEADS, TS_PRE, HEAD_PAD), lambda i, j, l: (i, 0, j, 0)),
                   pl.BlockSpec((1, N_HEADS, V_DIM, TS_PRE), lambda i, j, l: (i, 0, 0, j))],
    )
    return pl.pallas_call(
        _pre_kernel,
        grid_spec=grid_spec,
        out_shape=[jax.ShapeDtypeStruct((b, N_HEADS, HEAD_PAD, s), BF16),
                   jax.ShapeDtypeStruct((b, N_HEADS, s, HEAD_PAD), BF16),
                   jax.ShapeDtypeStruct((b, N_HEADS, V_DIM, s), BF16)],
        compiler_params=pltpu.CompilerParams(dimension_semantics=("parallel", "parallel"),
                                             vmem_limit_bytes=VMEM_LIMIT),
        name="pre_attn",
    )(layer, x, p["norm_g"], p["w1"], p["q_norm_g"], p["kv_norm_g"], p["w_uqT"], p["w_uk"], p["w_uvT"],
      p["qg"], p["qoff"], p["gn"], p["ga"], p["gb"], cosT, sinT, cosF, sinF)


def _attn_kernel(qT_ref, k_ref, vT_ref, oT_ref, s_scr):
    qT = qT_ref[0, 0]
    tq = qT.shape[1]
    n_groups = k_ref.shape[2] // TK

    def scores(g, slot):
        start = pl.multiple_of(g * TK, TK)
        s = jnp.dot(k_ref[0, 0, pl.ds(start, TK), :], qT, preferred_element_type=F32)
        s_scr[slot] = s
        return jnp.max(s, axis=0, keepdims=True)

    def accumulate(g, slot, m, l, acc, m_grp):
        start = pl.multiple_of(g * TK, TK)
        m_new = jnp.maximum(m, m_grp)
        alpha = jnp.exp2(m - m_new)
        p = jnp.exp2(s_scr[slot] - m_new)
        l = l * alpha + jnp.sum(p, axis=0, keepdims=True)
        pv = jnp.dot(vT_ref[0, 0, :, pl.ds(start, TK)], p.astype(BF16), preferred_element_type=F32)
        return m_new, l, acc * alpha + pv

    def pair(i, carry):
        m, l, acc, m_grp = carry
        g = 2 * i
        m_grp1 = scores(g + 1, 1)
        m, l, acc = accumulate(g, 0, m, l, acc, m_grp)
        m_grp2 = scores(g + 2, 0)
        m, l, acc = accumulate(g + 1, 1, m, l, acc, m_grp1)
        return m, l, acc, m_grp2

    init = (jnp.full((1, tq), -1e30, F32), jnp.zeros((1, tq), F32), jnp.zeros((V_DIM, tq), F32), scores(0, 0))
    m, l, acc, m_grp = lax.fori_loop(0, n_groups // 2 - 1, pair, init)
    g = n_groups - 2
    m_grp1 = scores(g + 1, 1)
    m, l, acc = accumulate(g, 0, m, l, acc, m_grp)
    m, l, acc = accumulate(g + 1, 1, m, l, acc, m_grp1)
    oT_ref[0, 0] = acc / l


def _attn_bounded_kernel(qT_ref, k_ref, vT_ref, oT_ref):
    n_chunks = k_ref.shape[2] // TKB
    n_tiles = qT_ref.shape[3] // TQB

    def query_tile(j, carry):
        q0 = pl.multiple_of(j * TQB, TQB)
        qT = qT_ref[0, 0, :, pl.ds(q0, TQB)]
        halves = [qT[:, h * TQH:(h + 1) * TQH] for h in range(TQB // TQH)]

        def scores(c, h):
            return jnp.dot(k_ref[0, 0, c * TKB:(c + 1) * TKB, :], halves[h], preferred_element_type=F32)

        n_h = len(halves)
        l8 = [jnp.zeros((SUBLANES, TQH), F32) for _ in range(n_h)]
        acc = [jnp.zeros((V_DIM, TQH), F32) for _ in range(n_h)]
        s_next = [scores(0, h) for h in range(n_h)]
        for c in range(n_chunks):
            vc = vT_ref[0, 0, :, c * TKB:(c + 1) * TKB]
            for h in range(n_h):
                s_cur = s_next[h]
                if c + 1 < n_chunks:
                    s_next[h] = scores(c + 1, h)
                p = jnp.exp2(s_cur)
                l8[h] = l8[h] + jnp.sum(p.reshape(TKB // SUBLANES, SUBLANES, TQH), axis=0)
                acc[h] = acc[h] + jnp.dot(vc, p.astype(BF16), preferred_element_type=F32)
        for h in range(n_h):
            oT_ref[0, 0, :, pl.ds(q0 + h * TQH, TQH)] = acc[h] / jnp.sum(l8[h], axis=0, keepdims=True)
        return carry

    lax.fori_loop(0, n_tiles, query_tile, 0)


def _attn_bounded_call(qT, k, vT):
    b, nh, _, s = qT.shape
    assert s % TKB == 0 and s % TQB == 0
    return pl.pallas_call(
        _attn_bounded_kernel,
        grid=(b, nh),
        in_specs=[pl.BlockSpec((1, 1, HEAD_PAD, s), lambda i, h: (i, h, 0, 0)),
                  pl.BlockSpec((1, 1, s, HEAD_PAD), lambda i, h: (i, h, 0, 0)),
                  pl.BlockSpec((1, 1, V_DIM, s), lambda i, h: (i, h, 0, 0))],
        out_specs=pl.BlockSpec((1, 1, V_DIM, s), lambda i, h: (i, h, 0, 0)),
        out_shape=jax.ShapeDtypeStruct((b, nh, V_DIM, s), F32),
        compiler_params=pltpu.CompilerParams(dimension_semantics=("parallel", "parallel"),
                                             vmem_limit_bytes=VMEM_LIMIT),
        name="mla_attn_bounded",
    )(qT, k, vT)


def _attn_call(qT, k, vT):
    b, nh, _, s = qT.shape
    assert (s // TK) % 2 == 0
    return pl.pallas_call(
        _attn_kernel,
        grid=(b, nh, s // TQ),
        in_specs=[pl.BlockSpec((1, 1, HEAD_PAD, TQ), lambda i, h, j: (i, h, 0, j)),
                  pl.BlockSpec((1, 1, s, HEAD_PAD), lambda i, h, j: (i, h, 0, 0)),
                  pl.BlockSpec((1, 1, V_DIM, s), lambda i, h, j: (i, h, 0, 0))],
        out_specs=pl.BlockSpec((1, 1, V_DIM, TQ), lambda i, h, j: (i, h, 0, j)),
        out_shape=jax.ShapeDtypeStruct((b, nh, V_DIM, s), F32),
        scratch_shapes=[pltpu.VMEM((2, TK, TQ), F32)],
        compiler_params=pltpu.CompilerParams(dimension_semantics=("parallel", "parallel", "arbitrary"),
                                             vmem_limit_bytes=VMEM_LIMIT),
        name="mla_attn",
    )(qT, k, vT)


def _post_kernel(layer_ref, x_ref, xp_ref, xn_ref, oT_ref, mk_ref, mv_ref, ng_ref, wr_ref, bg_ref,
                 cw_ref, cb_ref, mqg_ref, wa_ref, wc_ref, wm_ref, wo_ref, out_ref):
    del layer_ref
    j = pl.program_id(1)
    nj = pl.num_programs(1)
    x = x_ref[0]
    t = x.shape[0]
    ng = ng_ref[0]
    hb = (_rms(x) * ng).astype(BF16)
    xh = jnp.concatenate([xp_ref[0, 0], xn_ref[0, 0]], axis=0)
    hh = (_rms(xh) * ng).astype(BF16)
    h_ext = jnp.concatenate([hb, hh], axis=0)

    def proj(lhs, lo, hi):
        return jnp.dot(lhs, wr_ref[0, :, lo:hi], preferred_element_type=F32)

    ccu = proj(h_ext, R_CC, R_QM)
    z_ext = ccu[:, :CONV_W] * ccu[:, CONV_W:]
    z = z_ext[:t]
    z_before = z_ext[t + HALO - 1:t + HALO] * (j > 0).astype(F32)
    z_after = z_ext[t + HALO:t + HALO + 1] * (j < nj - 1).astype(F32)
    row = lax.broadcasted_iota(jnp.int32, z.shape, 0)
    z_prev = jnp.where(row == 0, z_before, pltpu.roll(z, 1, 0))
    z_next = jnp.where(row == t - 1, z_after, pltpu.roll(z, t - 1, 0))
    cw = cw_ref[0]
    conv = z_prev * cw[0:1] + z * cw[1:2] + z_next * cw[2:3] + cb_ref[0]
    o_conv = proj(hb, R_CB, R_CC) * conv * _silu(proj(hb, R_GC, R_GM))
    y_conv = jnp.dot(o_conv.astype(BF16), wc_ref[0], preferred_element_type=F32)

    oT = oT_ref[0]
    o_attn = oT.reshape(MLA_W, t).T * _silu(proj(hb, R_GA, R_GC))
    y_attn = jnp.dot(o_attn.astype(BF16), wa_ref[0], preferred_element_type=F32)

    qm = proj(hb, R_QM, R_GA)
    mqg = mqg_ref[0]
    heads = []
    for h in range(MEM_HEADS):
        qh = (_rms(qm[:, h * MEM_HD:(h + 1) * MEM_HD]) * mqg).astype(BF16)
        s = lax.dot_general(qh, mk_ref[0, 0, h], NT_DIMS, preferred_element_type=F32)
        s = s * (MEM_HD ** -0.5)
        p = jnp.exp(s - jnp.max(s, axis=-1, keepdims=True))
        l = jnp.sum(p, axis=-1, keepdims=True)
        oh = jnp.dot(p.astype(BF16), mv_ref[0, 0, h], preferred_element_type=F32)
        heads.append(oh / l)
    o_mem = jnp.concatenate(heads, axis=-1) * _silu(proj(hb, R_GM, R_R))
    y_mem = jnp.dot(o_mem.astype(BF16), wm_ref[0], preferred_element_type=F32)

    bg = bg_ref[0]
    r_a = _sigmoid(proj(hb, R_R, R_R + D_MODEL) + bg[:, :D_MODEL])
    y = r_a * y_attn
    r_c = _sigmoid(proj(hb, R_R + D_MODEL, R_R + 2 * D_MODEL) + bg[:, D_MODEL:2 * D_MODEL])
    y = y + r_c * y_conv
    r_m = _sigmoid(proj(hb, R_R + 2 * D_MODEL, R_R + 3 * D_MODEL) + bg[:, 2 * D_MODEL:])
    y = y + r_m * y_mem
    out_ref[0] = x + jnp.dot(y.astype(BF16), wo_ref[0], preferred_element_type=F32)


def _post_call(layer, x, oT, mk, mv, p):
    b, s, d = x.shape
    m = mk.shape[3]
    nblk = TS_POST // HALO
    last = s // HALO - 1
    x4 = x.reshape(b, s // HALO, HALO, d)
    wspec = lambda shape: pl.BlockSpec((1,) + shape, lambda i, j, l: (l[0],) + (0,) * len(shape),
                                       pipeline_mode=pl.Buffered(1))
    grid_spec = pltpu.PrefetchScalarGridSpec(
        num_scalar_prefetch=1,
        grid=(b, s // TS_POST),
        in_specs=[pl.BlockSpec((1, TS_POST, d), lambda i, j, l: (i, j, 0)),
                  pl.BlockSpec((1, 1, HALO, d), lambda i, j, l: (i, jnp.maximum(j * nblk - 1, 0), 0, 0)),
                  pl.BlockSpec((1, 1, HALO, d), lambda i, j, l: (i, jnp.minimum((j + 1) * nblk, last), 0, 0)),
                  pl.BlockSpec((1, N_HEADS, V_DIM, TS_POST), lambda i, j, l: (i, 0, 0, j)),
                  pl.BlockSpec((1, 1, MEM_HEADS, m, MEM_HD), lambda i, j, l: (l[0], i, 0, 0, 0)),
                  pl.BlockSpec((1, 1, MEM_HEADS, m, MEM_HD), lambda i, j, l: (l[0], i, 0, 0, 0)),
                  wspec((1, d)), wspec((d, REST_COLS)), wspec((1, N_BRANCH * d)),
                  wspec((3, CONV_W)), wspec((1, CONV_W)), wspec((1, MEM_HD)),
                  wspec((MLA_W, d)), wspec((CONV_W, d)), wspec((MEM_W, d)), wspec((d, d))],
        out_specs=pl.BlockSpec((1, TS_POST, d), lambda i, j, l: (i, j, 0)),
    )
    return pl.pallas_call(
        _post_kernel,
        grid_spec=grid_spec,
        out_shape=jax.ShapeDtypeStruct((b, s, d), F32),
        compiler_params=pltpu.CompilerParams(dimension_semantics=("parallel", "parallel"),
                                             vmem_limit_bytes=VMEM_LIMIT),
        name="post_attn",
    )(layer, x, x4, x4, oT, mk, mv, p["norm_g"], p["w_rest"], p["b_gate"], p["conv_w"], p["conv_b"],
      p["mem_q_g"], p["w_br_attn"], p["w_br_conv"], p["w_br_mem"], p["w_out"])


def _prepare_params(norm_g, w_in, b_gate, q_norm_g, w_uq, kv_norm_g, w_ukv, q_head_g, k_head_g,
                    conv_w, conv_b, mem_q_g, w_br_attn, w_br_conv, w_br_mem, w_out):
    depth = w_in.shape[0]
    d = D_MODEL
    half = ROPE // 2
    z = lambda n: jnp.zeros((depth, d, n), F32)
    t1 = w_in[:, :, OFF_KPE:OFF_KPE + half]
    t2 = w_in[:, :, OFF_KPE + half:OFF_REST]
    pad = HEAD_PAD - QK_DIM
    w1 = jnp.concatenate([w_in[:, :, :OFF_KPE], z(NOPE), t1, t2, z(pad), z(NOPE), t2, t1, z(pad)], axis=-1)

    w_uq4 = w_uq.reshape(depth, Q_RANK, N_HEADS, QK_DIM)
    w_uq_pad = jnp.pad(w_uq4, ((0, 0), (0, 0), (0, 0), (0, pad))).reshape(depth, Q_RANK, N_HEADS * HEAD_PAD)
    w_ukv4 = w_ukv.reshape(depth, KV_RANK, N_HEADS, NOPE + V_DIM)
    w_uk = jnp.pad(w_ukv4[..., :NOPE], ((0, 0), (0, 0), (0, 0), (0, HEAD_PAD - NOPE)))
    w_uv = w_ukv4[..., NOPE:].reshape(depth, KV_RANK, N_HEADS * V_DIM)

    zg = lambda n: jnp.zeros((depth, n), F32)
    q_scale = (QK_DIM ** -0.5) * LOG2E
    qg = jnp.concatenate([q_head_g * q_scale, zg(pad)], axis=-1)[..., None]
    score_bound = (QK_DIM ** 0.5) * LOG2E * jnp.max(jnp.abs(q_head_g), axis=-1) * jnp.max(jnp.abs(k_head_g), axis=-1)
    bounded = score_bound <= MAX_SCORE_BOUND
    off = jnp.where(bounded, -score_bound, 0.0)
    qoff = jnp.concatenate([zg(QK_DIM), jnp.broadcast_to(off[:, None], (depth, pad))], axis=-1)[..., None]
    g_nope, g1, g2 = k_head_g[:, :NOPE], k_head_g[:, NOPE:NOPE + half], k_head_g[:, NOPE + half:]
    gn = jnp.concatenate([g_nope, zg(HEAD_PAD - NOPE)], axis=-1)[:, None, :]
    ga = jnp.concatenate([zg(NOPE), g1, g2, zg(pad)], axis=-1)[:, None, :]
    gb = jnp.concatenate([zg(NOPE), g2, g1, zg(pad)], axis=-1)[:, None, :]
    return dict(
        norm_g=norm_g[:, None, :],
        w1=w1.astype(BF16),
        q_norm_g=q_norm_g[:, None, :],
        kv_norm_g=kv_norm_g[:, None, :],
        w_uqT=jnp.swapaxes(w_uq_pad, 1, 2).astype(BF16),
        w_uk=w_uk.reshape(depth, KV_RANK, N_HEADS * HEAD_PAD).astype(BF16),
        w_uvT=jnp.swapaxes(w_uv, 1, 2).astype(BF16),
        qg=qg, qoff=qoff, bounded=bounded, gn=gn, ga=ga, gb=gb,
        w_rest=w_in[:, :, OFF_REST:].astype(BF16),
        b_gate=b_gate[:, None, :],
        conv_w=conv_w,
        conv_b=conv_b[:, None, :],
        mem_q_g=mem_q_g[:, None, :],
        w_br_attn=w_br_attn.astype(BF16),
        w_br_conv=w_br_conv.astype(BF16),
        w_br_mem=w_br_mem.astype(BF16),
        w_out=w_out.astype(BF16),
    )


def kernel(x, mem, positions, norm_g, w_in, b_gate, q_norm_g, w_uq, kv_norm_g, w_ukv, q_head_g, k_head_g,
           conv_w, conv_b, mem_norm_g, w_mkv, mem_q_g, mem_k_g, w_br_attn, w_br_conv, w_br_mem, w_out):
    depth = w_in.shape[0]
    assert x.shape[-1] == D_MODEL and w_in.shape[-1] == OFF_REST + REST_COLS
    assert x.shape[1] % max(TS_PRE, TS_POST, TQ, TK, TS_TAB) == 0
    p = _prepare_params(norm_g, w_in, b_gate, q_norm_g, w_uq, kv_norm_g, w_ukv, q_head_g, k_head_g,
                        conv_w, conv_b, mem_q_g, w_br_attn, w_br_conv, w_br_mem, w_out)
    invf = ROPE_BASE ** (-jnp.arange(0, ROPE, 2, dtype=F32) / ROPE)
    tabs = _rope_tables(positions, invf[:, None])
    mk, mv = _mem_kv(mem, mem_norm_g, w_mkv.astype(BF16), mem_k_g)

    for i in range(depth):
        layer = jnp.full((1,), i, jnp.int32)
        qT, k, vT = _pre_call(layer, x, p, tabs)
        oT = lax.cond(p["bounded"][i], _attn_bounded_call, _attn_call, qT, k, vT)
        x = _post_call(layer, x, oT, mk, mv, p)
    return x
```

```python
import math

import jax
import jax.numpy as jnp
from jax import lax
from jax.experimental import pallas as pl
from jax.experimental.pallas import tpu as pltpu

D_MODEL = 1024
N_HEADS = 8
NOPE = 64
ROPE = 32
QK_DIM = NOPE + ROPE
V_DIM = 64
Q_RANK = 3 * D_MODEL // 8
KV_RANK = D_MODEL // 4
MLA_W = N_HEADS * V_DIM
CONV_W = D_MODEL // 2
MEM_HEADS = 4
MEM_HD = 128
MEM_W = MEM_HEADS * MEM_HD
N_BRANCH = 3
ROPE_BASE = 10000.0
EPS = 1e-6
LOG2E = math.log2(math.e)

LANES = 128
SUBLANES = 8
HEAD_PAD = LANES
VMEM_LIMIT = 56 * 1024 * 1024

OFF_KVLAT = Q_RANK
OFF_KPE = Q_RANK + KV_RANK
OFF_REST = OFF_KPE + ROPE
W1_COLS = OFF_KPE + 2 * HEAD_PAD
R_CB, R_CC, R_CU, R_QM, R_GA, R_GC, R_GM, R_R = (0, 512, 1024, 1536, 2048, 2560, 3072, 3584)
REST_COLS = R_R + N_BRANCH * D_MODEL

TS_TAB = 512
TS_PRE = 1024
TS_PRE_SUB = 256
TS_POST = 512
TQ = 256
TK = 1024
HALO = SUBLANES
TQB = 1024
TQH = 512
TKB = 256
MAX_SCORE_BOUND = 48.0

F32 = jnp.float32
BF16 = jnp.bfloat16
NT_DIMS = (((1,), (1,)), ((), ()))


def _rms(t):
    return t * lax.rsqrt(jnp.mean(t * t, axis=-1, keepdims=True) + EPS)


def _sigmoid(t):
    return 1.0 / (1.0 + jnp.exp(-t))


def _silu(t):
    return t * _sigmoid(t)


def _tables_kernel(pos_ref, invf_ref, cosT_ref, sinT_ref, cosF_ref, sinF_ref):
    pos = pos_ref[0].astype(F32)
    ang = invf_ref[...] * pos
    c = jnp.cos(ang)
    s = jnp.sin(ang)
    cosT_ref[0] = c
    sinT_ref[0] = s
    t = pos.shape[1]
    z64 = jnp.zeros((NOPE, t), F32)
    z32 = jnp.zeros((HEAD_PAD - QK_DIM, t), F32)
    cosF_ref[0] = jnp.concatenate([z64, c, c, z32], axis=0).T
    sinF_ref[0] = jnp.concatenate([z64, -s, s, z32], axis=0).T


def _rope_tables(positions, invf):
    b, s = positions.shape
    half = ROPE // 2
    return pl.pallas_call(
        _tables_kernel,
        grid=(b, s // TS_TAB),
        in_specs=[pl.BlockSpec((1, 1, TS_TAB), lambda i, j: (i, 0, j)),
                  pl.BlockSpec((half, 1), lambda i, j: (0, 0))],
        out_specs=[pl.BlockSpec((1, half, TS_TAB), lambda i, j: (i, 0, j)),
                   pl.BlockSpec((1, half, TS_TAB), lambda i, j: (i, 0, j)),
                   pl.BlockSpec((1, TS_TAB, HEAD_PAD), lambda i, j: (i, j, 0)),
                   pl.BlockSpec((1, TS_TAB, HEAD_PAD), lambda i, j: (i, j, 0))],
        out_shape=[jax.ShapeDtypeStruct((b, half, s), F32),
                   jax.ShapeDtypeStruct((b, half, s), F32),
                   jax.ShapeDtypeStruct((b, s, HEAD_PAD), F32),
                   jax.ShapeDtypeStruct((b, s, HEAD_PAD), F32)],
        compiler_params=pltpu.CompilerParams(dimension_semantics=("parallel", "parallel")),
        name="rope_tables",
    )(positions.reshape(b, 1, s), invf)


def _mem_kernel(mem_ref, g_ref, w_ref, kg_ref, mk_ref, mv_ref):
    m = mem_ref[0]
    mn = (_rms(m) * g_ref[0]).astype(BF16)
    mkv = jnp.dot(mn, w_ref[0], preferred_element_type=F32)
    for h in range(MEM_HEADS):
        kh = mkv[:, h * 2 * MEM_HD: h * 2 * MEM_HD + MEM_HD]
        vh = mkv[:, h * 2 * MEM_HD + MEM_HD: (h + 1) * 2 * MEM_HD]
        mk_ref[0, 0, h] = (_rms(kh) * kg_ref[0]).astype(BF16)
        mv_ref[0, 0, h] = vh.astype(BF16)


def _mem_kv(mem, mem_norm_g, w_mkv_bf, mem_k_g):
    depth = w_mkv_bf.shape[0]
    b, m, d = mem.shape
    return pl.pallas_call(
        _mem_kernel,
        grid=(depth, b),
        in_specs=[pl.BlockSpec((1, m, d), lambda l, i: (i, 0, 0)),
                  pl.BlockSpec((1, 1, d), lambda l, i: (l, 0, 0)),
                  pl.BlockSpec((1, d, 2 * MEM_W), lambda l, i: (l, 0, 0)),
                  pl.BlockSpec((1, 1, MEM_HD), lambda l, i: (l, 0, 0))],
        out_specs=[pl.BlockSpec((1, 1, MEM_HEADS, m, MEM_HD), lambda l, i: (l, i, 0, 0, 0)),
                   pl.BlockSpec((1, 1, MEM_HEADS, m, MEM_HD), lambda l, i: (l, i, 0, 0, 0))],
        out_shape=[jax.ShapeDtypeStruct((depth, b, MEM_HEADS, m, MEM_HD), BF16),
                   jax.ShapeDtypeStruct((depth, b, MEM_HEADS, m, MEM_HD), BF16)],
        compiler_params=pltpu.CompilerParams(dimension_semantics=("parallel", "parallel")),
        name="mem_kv",
    )(mem, mem_norm_g.reshape(depth, 1, d), w_mkv_bf, mem_k_g.reshape(depth, 1, MEM_HD))


def _pre_kernel(layer_ref, x_ref, ng_ref, w1_ref, qng_ref, kvng_ref, wuq_ref, wuk_ref, wuv_ref,
                qg_ref, qoff_ref, gn_ref, ga_ref, gb_ref, cosT_ref, sinT_ref, cosF_ref, sinF_ref,
                qT_ref, k_ref, vT_ref):
    del layer_ref
    half = ROPE // 2
    qg = qg_ref[0]
    qoff = qoff_ref[0]
    gn = gn_ref[0]
    one_lane = (lax.broadcasted_iota(jnp.int32, (1, HEAD_PAD), 1) == QK_DIM).astype(F32)
    subs = [slice(i * TS_PRE_SUB, (i + 1) * TS_PRE_SUB) for i in range(x_ref.shape[1] // TS_PRE_SUB)]

    a_subs = []
    for r in subs:
        hb = (_rms(x_ref[0, r, :]) * ng_ref[0]).astype(BF16)
        a_subs.append(jnp.dot(hb, w1_ref[0], preferred_element_type=F32))

    for r, a in zip(subs, a_subs):
        qn = (_rms(a[:, :Q_RANK]) * qng_ref[0]).astype(BF16)
        kvn = (_rms(a[:, OFF_KVLAT:OFF_KPE]) * kvng_ref[0]).astype(BF16)
        pe = a[:, OFF_KPE:OFF_KPE + HEAD_PAD]
        pe_sw = a[:, OFF_KPE + HEAD_PAD:W1_COLS]
        qT = lax.dot_general(wuq_ref[0], qn, NT_DIMS, preferred_element_type=F32)
        knope = jnp.dot(kvn, wuk_ref[0], preferred_element_type=F32)
        vT = lax.dot_general(wuv_ref[0], kvn, NT_DIMS, preferred_element_type=F32)

        cT = cosT_ref[0, :, r]
        sT = sinT_ref[0, :, r]
        for h in range(N_HEADS):
            qh = qT[h * HEAD_PAD:(h + 1) * HEAD_PAD]
            ms = jnp.sum(qh * qh, axis=0, keepdims=True) * (1.0 / QK_DIM)
            qh = qh * lax.rsqrt(ms + EPS) * qg + qoff
            t1 = qh[NOPE:NOPE + half]
            t2 = qh[NOPE + half:QK_DIM]
            qh = jnp.concatenate([qh[:NOPE], t1 * cT - t2 * sT, t2 * cT + t1 * sT, qh[QK_DIM:]], axis=0)
            qT_ref[0, h, :, r] = qh.astype(BF16)

        u = pe * (ga_ref[0] * cosF_ref[0, r, :]) + pe_sw * (gb_ref[0] * sinF_ref[0, r, :])
        ss_pe = jnp.sum(pe * pe, axis=-1, keepdims=True)
        for h in range(N_HEADS):
            kn = knope[:, h * HEAD_PAD:(h + 1) * HEAD_PAD]
            ms = (jnp.sum(kn * kn, axis=-1, keepdims=True) + ss_pe) * (1.0 / QK_DIM)
            k_ref[0, h, r, :] = ((kn * gn + u) * lax.rsqrt(ms + EPS) + one_lane).astype(BF16)

        vT_ref[0, :, :, r] = vT.reshape(N_HEADS, V_DIM, vT.shape[-1]).astype(BF16)


def _pre_call(layer, x, p, tabs):
    b, s, d = x.shape
    cosT, sinT, cosF, sinF = tabs
    half = ROPE // 2
    wspec = lambda shape: pl.BlockSpec((1,) + shape, lambda i, j, l: (l[0],) + (0,) * len(shape))
    grid_spec = pltpu.PrefetchScalarGridSpec(
        num_scalar_prefetch=1,
        grid=(b, s // TS_PRE),
        in_specs=[pl.BlockSpec((1, TS_PRE, d), lambda i, j, l: (i, j, 0)),
                  wspec((1, d)), wspec((d, W1_COLS)), wspec((1, Q_RANK)), wspec((1, KV_RANK)),
                  wspec((N_HEADS * HEAD_PAD, Q_RANK)), wspec((KV_RANK, N_HEADS * HEAD_PAD)),
                  wspec((N_HEADS * V_DIM, KV_RANK)),
                  wspec((HEAD_PAD, 1)), wspec((HEAD_PAD, 1)),
                  wspec((1, HEAD_PAD)), wspec((1, HEAD_PAD)), wspec((1, HEAD_PAD)),
                  pl.BlockSpec((1, half, TS_PRE), lambda i, j, l: (i, 0, j)),
                  pl.BlockSpec((1, half, TS_PRE), lambda i, j, l: (i, 0, j)),
                  pl.BlockSpec((1, TS_PRE, HEAD_PAD), lambda i, j, l: (i, j, 0)),
                  pl.BlockSpec((1, TS_PRE, HEAD_PAD), lambda i, j, l: (i, j, 0))],
        out_specs=[pl.BlockSpec((1, N_HEADS, HEAD_PAD, TS_PRE), lambda i, j, l: (i, 0, 0, j)),
                   pl.BlockSpec((1, N_HEADS, TS_PRE, HEAD_PAD), lambda i, j, l: (i, 0, j, 0)),
                   pl.BlockSpec((1, N_HEADS, V_DIM, TS_PRE), lambda i, j, l: (i, 0, 0, j))],
    )
    return pl.pallas_call(
        _pre_kernel,
        grid_spec=grid_spec,
        out_shape=[jax.ShapeDtypeStruct((b, N_HEADS, HEAD_PAD, s), BF16),
                   jax.ShapeDtypeStruct((b, N_HEADS, s, HEAD_PAD), BF16),
                   jax.ShapeDtypeStruct((b, N_HEADS, V_DIM, s), BF16)],
        compiler_params=pltpu.CompilerParams(dimension_semantics=("parallel", "parallel"),
                                             vmem_limit_bytes=VMEM_LIMIT),
        name="pre_attn",
    )(layer, x, p["norm_g"], p["w1"], p["q_norm_g"], p["kv_norm_g"], p["w_uqT"], p["w_uk"], p["w_uvT"],
      p["qg"], p["qoff"], p["gn"], p["ga"], p["gb"], cosT, sinT, cosF, sinF)


def _attn_kernel(qT_ref, k_ref, vT_ref, oT_ref, s_scr):
    qT = qT_ref[0, 0]
    tq = qT.shape[1]
    n_groups = k_ref.shape[2] // TK

    def scores(g, slot):
        start = pl.multiple_of(g * TK, TK)
        s = jnp.dot(k_ref[0, 0, pl.ds(start, TK), :], qT, preferred_element_type=F32)
        s_scr[slot] = s
        return jnp.max(s, axis=0, keepdims=True)

    def accumulate(g, slot, m, l, acc, m_grp):
        start = pl.multiple_of(g * TK, TK)
        m_new = jnp.maximum(m, m_grp)
        alpha = jnp.exp2(m - m_new)
        p = jnp.exp2(s_scr[slot] - m_new)
        l = l * alpha + jnp.sum(p, axis=0, keepdims=True)
        pv = jnp.dot(vT_ref[0, 0, :, pl.ds(start, TK)], p.astype(BF16), preferred_element_type=F32)
        return m_new, l, acc * alpha + pv

    def pair(i, carry):
        m, l, acc, m_grp = carry
        g = 2 * i
        m_grp1 = scores(g + 1, 1)
        m, l, acc = accumulate(g, 0, m, l, acc, m_grp)
        m_grp2 = scores(g + 2, 0)
        m, l, acc = accumulate(g + 1, 1, m, l, acc, m_grp1)
        return m, l, acc, m_grp2

    init = (jnp.full((1, tq), -1e30, F32), jnp.zeros((1, tq), F32), jnp.zeros((V_DIM, tq), F32), scores(0, 0))
    m, l, acc, m_grp = lax.fori_loop(0, n_groups // 2 - 1, pair, init)
    g = n_groups - 2
    m_grp1 = scores(g + 1, 1)
    m, l, acc = accumulate(g, 0, m, l, acc, m_grp)
    m, l, acc = accumulate(g + 1, 1, m, l, acc, m_grp1)
    oT_ref[0, 0] = acc / l


def _attn_bounded_kernel(qT_ref, k_ref, vT_ref, oT_ref):
    n_chunks = k_ref.shape[2] // TKB
    n_tiles = qT_ref.shape[3] // TQB

    def query_tile(j, carry):
        q0 = pl.multiple_of(j * TQB, TQB)
        qT = qT_ref[0, 0, :, pl.ds(q0, TQB)]
        halves = [qT[:, h * TQH:(h + 1) * TQH] for h in range(TQB // TQH)]

        def scores(c, h):
            return jnp.dot(k_ref[0, 0, c * TKB:(c + 1) * TKB, :], halves[h], preferred_element_type=F32)

        n_h = len(halves)
        l8 = [jnp.zeros((SUBLANES, TQH), F32) for _ in range(n_h)]
        acc = [jnp.zeros((V_DIM, TQH), F32) for _ in range(n_h)]
        s_next = [scores(0, h) for h in range(n_h)]
        for c in range(n_chunks):
            vc = vT_ref[0, 0, :, c * TKB:(c + 1) * TKB]
            for h in range(n_h):
                s_cur = s_next[h]
                if c + 1 < n_chunks:
                    s_next[h] = scores(c + 1, h)
                p = jnp.exp2(s_cur)
                l8[h] = l8[h] + jnp.sum(p.reshape(TKB // SUBLANES, SUBLANES, TQH), axis=0)
                acc[h] = acc[h] + jnp.dot(vc, p.astype(BF16), preferred_element_type=F32)
        for h in range(n_h):
            oT_ref[0, 0, :, pl.ds(q0 + h * TQH, TQH)] = acc[h] / jnp.sum(l8[h], axis=0, keepdims=True)
        return carry

    lax.fori_loop(0, n_tiles, query_tile, 0)


def _attn_bounded_call(qT, k, vT):
    b, nh, _, s = qT.shape
    assert s % TKB == 0 and s % TQB == 0
    return pl.pallas_call(
        _attn_bounded_kernel,
        grid=(b, nh),
        in_specs=[pl.BlockSpec((1, 1, HEAD_PAD, s), lambda i, h: (i, h, 0, 0)),
                  pl.BlockSpec((1, 1, s, HEAD_PAD), lambda i, h: (i, h, 0, 0)),
                  pl.BlockSpec((1, 1, V_DIM, s), lambda i, h: (i, h, 0, 0))],
        out_specs=pl.BlockSpec((1, 1, V_DIM, s), lambda i, h: (i, h, 0, 0)),
        out_shape=jax.ShapeDtypeStruct((b, nh, V_DIM, s), F32),
        compiler_params=pltpu.CompilerParams(dimension_semantics=("parallel", "parallel"),
                                             vmem_limit_bytes=VMEM_LIMIT),
        name="mla_attn_bounded",
    )(qT, k, vT)


def _attn_call(qT, k, vT):
    b, nh, _, s = qT.shape
    assert (s // TK) % 2 == 0
    return pl.pallas_call(
        _attn_kernel,
        grid=(b, nh, s // TQ),
        in_specs=[pl.BlockSpec((1, 1, HEAD_PAD, TQ), lambda i, h, j: (i, h, 0, j)),
                  pl.BlockSpec((1, 1, s, HEAD_PAD), lambda i, h, j: (i, h, 0, 0)),
                  pl.BlockSpec((1, 1, V_DIM, s), lambda i, h, j: (i, h, 0, 0))],
        out_specs=pl.BlockSpec((1, 1, V_DIM, TQ), lambda i, h, j: (i, h, 0, j)),
        out_shape=jax.ShapeDtypeStruct((b, nh, V_DIM, s), F32),
        scratch_shapes=[pltpu.VMEM((2, TK, TQ), F32)],
        compiler_params=pltpu.CompilerParams(dimension_semantics=("parallel", "parallel", "arbitrary"),
                                             vmem_limit_bytes=VMEM_LIMIT),
        name="mla_attn",
    )(qT, k, vT)


def _post_kernel(layer_ref, x_ref, xp_ref, xn_ref, oT_ref, mk_ref, mv_ref, ng_ref, wr_ref, bg_ref,
                 cw_ref, cb_ref, mqg_ref, wa_ref, wc_ref, wm_ref, wo_ref, out_ref):
    del layer_ref
    j = pl.program_id(1)
    nj = pl.num_programs(1)
    x = x_ref[0]
    t = x.shape[0]
    ng = ng_ref[0]
    hb = (_rms(x) * ng).astype(BF16)
    xh = jnp.concatenate([xp_ref[0, 0], xn_ref[0, 0]], axis=0)
    hh = (_rms(xh) * ng).astype(BF16)
    h_ext = jnp.concatenate([hb, hh], axis=0)

    def proj(lhs, lo, hi):
        return jnp.dot(lhs, wr_ref[0, :, lo:hi], preferred_element_type=F32)

    ccu = proj(h_ext, R_CC, R_QM)
    z_ext = ccu[:, :CONV_W] * ccu[:, CONV_W:]
    z = z_ext[:t]
    z_before = z_ext[t + HALO - 1:t + HALO] * (j > 0).astype(F32)
    z_after = z_ext[t + HALO:t + HALO + 1] * (j < nj - 1).astype(F32)
    row = lax.broadcasted_iota(jnp.int32, z.shape, 0)
    z_prev = jnp.where(row == 0, z_before, pltpu.roll(z, 1, 0))
    z_next = jnp.where(row == t - 1, z_after, pltpu.roll(z, t - 1, 0))
    cw = cw_ref[0]
    conv = z_prev * cw[0:1] + z * cw[1:2] + z_next * cw[2:3] + cb_ref[0]
    o_conv = proj(hb, R_CB, R_CC) * conv * _silu(proj(hb, R_GC, R_GM))
    y_conv = jnp.dot(o_conv.astype(BF16), wc_ref[0], preferred_element_type=F32)

    oT = oT_ref[0]
    o_attn = oT.reshape(MLA_W, t).T * _silu(proj(hb, R_GA, R_GC))
    y_attn = jnp.dot(o_attn.astype(BF16), wa_ref[0], preferred_element_type=F32)

    qm = proj(hb, R_QM, R_GA)
    mqg = mqg_ref[0]
    heads = []
    for h in range(MEM_HEADS):
        qh = (_rms(qm[:, h * MEM_HD:(h + 1) * MEM_HD]) * mqg).astype(BF16)
        s = lax.dot_general(qh, mk_ref[0, 0, h], NT_DIMS, preferred_element_type=F32)
        s = s * (MEM_HD ** -0.5)
        p = jnp.exp(s - jnp.max(s, axis=-1, keepdims=True))
        l = jnp.sum(p, axis=-1, keepdims=True)
        oh = jnp.dot(p.astype(BF16), mv_ref[0, 0, h], preferred_element_type=F32)
        heads.append(oh / l)
    o_mem = jnp.concatenate(heads, axis=-1) * _silu(proj(hb, R_GM, R_R))
    y_mem = jnp.dot(o_mem.astype(BF16), wm_ref[0], preferred_element_type=F32)

    bg = bg_ref[0]
    r_a = _sigmoid(proj(hb, R_R, R_R + D_MODEL) + bg[:, :D_MODEL])
    y = r_a * y_attn
    r_c = _sigmoid(proj(hb, R_R + D_MODEL, R_R + 2 * D_MODEL) + bg[:, D_MODEL:2 * D_MODEL])
    y = y + r_c * y_conv
    r_m = _sigmoid(proj(hb, R_R + 2 * D_MODEL, R_R + 3 * D_MODEL) + bg[:, 2 * D_MODEL:])
    y = y + r_m * y_mem
    out_ref[0] = x + jnp.dot(y.astype(BF16), wo_ref[0], preferred_element_type=F32)


def _post_call(layer, x, oT, mk, mv, p):
    b, s, d = x.shape
    m = mk.shape[3]
    nblk = TS_POST // HALO
    last = s // HALO - 1
    x4 = x.reshape(b, s // HALO, HALO, d)
    wspec = lambda shape: pl.BlockSpec((1,) + shape, lambda i, j, l: (l[0],) + (0,) * len(shape),
                                       pipeline_mode=pl.Buffered(1))
    grid_spec = pltpu.PrefetchScalarGridSpec(
        num_scalar_prefetch=1,
        grid=(b, s // TS_POST),
        in_specs=[pl.BlockSpec((1, TS_POST, d), lambda i, j, l: (i, j, 0)),
                  pl.BlockSpec((1, 1, HALO, d), lambda i, j, l: (i, jnp.maximum(j * nblk - 1, 0), 0, 0)),
                  pl.BlockSpec((1, 1, HALO, d), lambda i, j, l: (i, jnp.minimum((j + 1) * nblk, last), 0, 0)),
                  pl.BlockSpec((1, N_HEADS, V_DIM, TS_POST), lambda i, j, l: (i, 0, 0, j)),
                  pl.BlockSpec((1, 1, MEM_HEADS, m, MEM_HD), lambda i, j, l: (l[0], i, 0, 0, 0)),
                  pl.BlockSpec((1, 1, MEM_HEADS, m, MEM_HD), lambda i, j, l: (l[0], i, 0, 0, 0)),
                  wspec((1, d)), wspec((d, REST_COLS)), wspec((1, N_BRANCH * d)),
                  wspec((3, CONV_W)), wspec((1, CONV_W)), wspec((1, MEM_HD)),
                  wspec((MLA_W, d)), wspec((CONV_W, d)), wspec((MEM_W, d)), wspec((d, d))],
        out_specs=pl.BlockSpec((1, TS_POST, d), lambda i, j, l: (i, j, 0)),
    )
    return pl.pallas_call(
        _post_kernel,
        grid_spec=grid_spec,
        out_shape=jax.ShapeDtypeStruct((b, s, d), F32),
        compiler_params=pltpu.CompilerParams(dimension_semantics=("parallel", "parallel"),
                                             vmem_limit_bytes=VMEM_LIMIT),
        name="post_attn",
    )(layer, x, x4, x4, oT, mk, mv, p["norm_g"], p["w_rest"], p["b_gate"], p["conv_w"], p["conv_b"],
      p["mem_q_g"], p["w_br_attn"], p["w_br_conv"], p["w_br_mem"], p["w_out"])


def _prepare_params(norm_g, w_in, b_gate, q_norm_g, w_uq, kv_norm_g, w_ukv, q_head_g, k_head_g,
                    conv_w, conv_b, mem_q_g, w_br_attn, w_br_conv, w_br_mem, w_out):
    depth = w_in.shape[0]
    d = D_MODEL
    half = ROPE // 2
    z = lambda n: jnp.zeros((depth, d, n), F32)
    t1 = w_in[:, :, OFF_KPE:OFF_KPE + half]
    t2 = w_in[:, :, OFF_KPE + half:OFF_REST]
    pad = HEAD_PAD - QK_DIM
    w1 = jnp.concatenate([w_in[:, :, :OFF_KPE], z(NOPE), t1, t2, z(pad), z(NOPE), t2, t1, z(pad)], axis=-1)

    w_uq4 = w_uq.reshape(depth, Q_RANK, N_HEADS, QK_DIM)
    w_uq_pad = jnp.pad(w_uq4, ((0, 0), (0, 0), (0, 0), (0, pad))).reshape(depth, Q_RANK, N_HEADS * HEAD_PAD)
    w_ukv4 = w_ukv.reshape(depth, KV_RANK, N_HEADS, NOPE + V_DIM)
    w_uk = jnp.pad(w_ukv4[..., :NOPE], ((0, 0), (0, 0), (0, 0), (0, HEAD_PAD - NOPE)))
    w_uv = w_ukv4[..., NOPE:].reshape(depth, KV_RANK, N_HEADS * V_DIM)

    zg = lambda n: jnp.zeros((depth, n), F32)
    q_scale = (QK_DIM ** -0.5) * LOG2E
    qg = jnp.concatenate([q_head_g * q_scale, zg(pad)], axis=-1)[..., None]
    score_bound = (QK_DIM ** 0.5) * LOG2E * jnp.max(jnp.abs(q_head_g), axis=-1) * jnp.max(jnp.abs(k_head_g), axis=-1)
    bounded = score_bound <= MAX_SCORE_BOUND
    off = jnp.where(bounded, -score_bound, 0.0)
    qoff = jnp.concatenate([zg(QK_DIM), jnp.broadcast_to(off[:, None], (depth, pad))], axis=-1)[..., None]
    g_nope, g1, g2 = k_head_g[:, :NOPE], k_head_g[:, NOPE:NOPE + half], k_head_g[:, NOPE + half:]
    gn = jnp.concatenate([g_nope, zg(HEAD_PAD - NOPE)], axis=-1)[:, None, :]
    ga = jnp.concatenate([zg(NOPE), g1, g2, zg(pad)], axis=-1)[:, None, :]
    gb = jnp.concatenate([zg(NOPE), g2, g1, zg(pad)], axis=-1)[:, None, :]
    return dict(
        norm_g=norm_g[:, None, :],
        w1=w1.astype(BF16),
        q_norm_g=q_norm_g[:, None, :],
        kv_norm_g=kv_norm_g[:, None, :],
        w_uqT=jnp.swapaxes(w_uq_pad, 1, 2).astype(BF16),
        w_uk=w_uk.reshape(depth, KV_RANK, N_HEADS * HEAD_PAD).astype(BF16),
        w_uvT=jnp.swapaxes(w_uv, 1, 2).astype(BF16),
        qg=qg, qoff=qoff, bounded=bounded, gn=gn, ga=ga, gb=gb,
        w_rest=w_in[:, :, OFF_REST:].astype(BF16),
        b_gate=b_gate[:, None, :],
        conv_w=conv_w,
        conv_b=conv_b[:, None, :],
        mem_q_g=mem_q_g[:, None, :],
        w_br_attn=w_br_attn.astype(BF16),
        w_br_conv=w_br_conv.astype(BF16),
        w_br_mem=w_br_mem.astype(BF16),
        w_out=w_out.astype(BF16),
    )


def kernel(x, mem, positions, norm_g, w_in, b_gate, q_norm_g, w_uq, kv_norm_g, w_ukv, q_head_g, k_head_g,
           conv_w, conv_b, mem_norm_g, w_mkv, mem_q_g, mem_k_g, w_br_attn, w_br_conv, w_br_mem, w_out):
    depth = w_in.shape[0]
    assert x.shape[-1] == D_MODEL and w_in.shape[-1] == OFF_REST + REST_COLS
    assert x.shape[1] % max(TS_PRE, TS_POST, TQ, TK, TS_TAB) == 0
    p = _prepare_params(norm_g, w_in, b_gate, q_norm_g, w_uq, kv_norm_g, w_ukv, q_head_g, k_head_g,
                        conv_w, conv_b, mem_q_g, w_br_attn, w_br_conv, w_br_mem, w_out)
    invf = ROPE_BASE ** (-jnp.arange(0, ROPE, 2, dtype=F32) / ROPE)
    tabs = _rope_tables(positions, invf[:, None])
    mk, mv = _mem_kv(mem, mem_norm_g, w_mkv.astype(BF16), mem_k_g)

    for i in range(depth):
        layer = jnp.full((1,), i, jnp.int32)
        qT, k, vT = _pre_call(layer, x, p, tabs)
        oT = lax.cond(p["bounded"][i], _attn_bounded_call, _attn_call, qT, k, vT)
        x = _post_call(layer, x, oT, mk, mv, p)
    return x
```

```python
import math

import jax
import jax.numpy as jnp
from jax import lax
from jax.experimental import pallas as pl
from jax.experimental.pallas import tpu as pltpu

D_MODEL = 1024
N_HEADS = 8
NOPE = 64
ROPE = 32
QK_DIM = NOPE + ROPE
V_DIM = 64
Q_RANK = 3 * D_MODEL // 8
KV_RANK = D_MODEL // 4
MLA_W = N_HEADS * V_DIM
CONV_W = D_MODEL // 2
MEM_HEADS = 4
MEM_HD = 128
MEM_W = MEM_HEADS * MEM_HD
N_BRANCH = 3
ROPE_BASE = 10000.0
EPS = 1e-6
LOG2E = math.log2(math.e)

LANES = 128
SUBLANES = 8
HEAD_PAD = LANES
VMEM_LIMIT = 56 * 1024 * 1024

OFF_KVLAT = Q_RANK
OFF_KPE = Q_RANK + KV_RANK
OFF_REST = OFF_KPE + ROPE
W1_COLS = OFF_KPE + 2 * HEAD_PAD
R_CB, R_CC, R_CU, R_QM, R_GA, R_GC, R_GM, R_R = (0, 512, 1024, 1536, 2048, 2560, 3072, 3584)
REST_COLS = R_R + N_BRANCH * D_MODEL

TS_TAB = 512
TS_PRE = 1024
TS_PRE_SUB = 256
TS_POST = 512
HALO = SUBLANES
TQB = 1024
TQH = 512
TKB = 256
SHIFT_ROWS = 2
MAX_SCORE_BOUND = 48.0

F32 = jnp.float32
BF16 = jnp.bfloat16
NT_DIMS = (((1,), (1,)), ((), ()))


def _rms(t):
    return t * lax.rsqrt(jnp.mean(t * t, axis=-1, keepdims=True) + EPS)


def _sigmoid(t):
    return 1.0 / (1.0 + jnp.exp(-t))


def _silu(t):
    return t * _sigmoid(t)


def _tables_kernel(pos_ref, invf_ref, cosT_ref, sinT_ref, cosF_ref, sinF_ref):
    pos = pos_ref[0].astype(F32)
    ang = invf_ref[...] * pos
    c = jnp.cos(ang)
    s = jnp.sin(ang)
    cosT_ref[0] = c
    sinT_ref[0] = s
    t = pos.shape[1]
    z64 = jnp.zeros((NOPE, t), F32)
    z32 = jnp.zeros((HEAD_PAD - QK_DIM, t), F32)
    cosF_ref[0] = jnp.concatenate([z64, c, c, z32], axis=0).T
    sinF_ref[0] = jnp.concatenate([z64, -s, s, z32], axis=0).T


def _rope_tables(positions, invf):
    b, s = positions.shape
    half = ROPE // 2
    return pl.pallas_call(
        _tables_kernel,
        grid=(b, s // TS_TAB),
        in_specs=[pl.BlockSpec((1, 1, TS_TAB), lambda i, j: (i, 0, j)),
                  pl.BlockSpec((half, 1), lambda i, j: (0, 0))],
        out_specs=[pl.BlockSpec((1, half, TS_TAB), lambda i, j: (i, 0, j)),
                   pl.BlockSpec((1, half, TS_TAB), lambda i, j: (i, 0, j)),
                   pl.BlockSpec((1, TS_TAB, HEAD_PAD), lambda i, j: (i, j, 0)),
                   pl.BlockSpec((1, TS_TAB, HEAD_PAD), lambda i, j: (i, j, 0))],
        out_shape=[jax.ShapeDtypeStruct((b, half, s), F32),
                   jax.ShapeDtypeStruct((b, half, s), F32),
                   jax.ShapeDtypeStruct((b, s, HEAD_PAD), F32),
                   jax.ShapeDtypeStruct((b, s, HEAD_PAD), F32)],
        compiler_params=pltpu.CompilerParams(dimension_semantics=("parallel", "parallel")),
        name="rope_tables",
    )(positions.reshape(b, 1, s), invf)


def _mem_kernel(mem_ref, g_ref, w_ref, kg_ref, mk_ref, mv_ref):
    m = mem_ref[0]
    mn = (_rms(m) * g_ref[0]).astype(BF16)
    mkv = jnp.dot(mn, w_ref[0], preferred_element_type=F32)
    for h in range(MEM_HEADS):
        kh = mkv[:, h * 2 * MEM_HD: h * 2 * MEM_HD + MEM_HD]
        vh = mkv[:, h * 2 * MEM_HD + MEM_HD: (h + 1) * 2 * MEM_HD]
        mk_ref[0, 0, h] = (_rms(kh) * kg_ref[0]).astype(BF16)
        mv_ref[0, 0, h] = vh.astype(BF16)


def _mem_kv(mem, mem_norm_g, w_mkv_bf, mem_k_g):
    depth = w_mkv_bf.shape[0]
    b, m, d = mem.shape
    return pl.pallas_call(
        _mem_kernel,
        grid=(depth, b),
        in_specs=[pl.BlockSpec((1, m, d), lambda l, i: (i, 0, 0)),
                  pl.BlockSpec((1, 1, d), lambda l, i: (l, 0, 0)),
                  pl.BlockSpec((1, d, 2 * MEM_W), lambda l, i: (l, 0, 0)),
                  pl.BlockSpec((1, 1, MEM_HD), lambda l, i: (l, 0, 0))],
        out_specs=[pl.BlockSpec((1, 1, MEM_HEADS, m, MEM_HD), lambda l, i: (l, i, 0, 0, 0)),
                   pl.BlockSpec((1, 1, MEM_HEADS, m, MEM_HD), lambda l, i: (l, i, 0, 0, 0))],
        out_shape=[jax.ShapeDtypeStruct((depth, b, MEM_HEADS, m, MEM_HD), BF16),
                   jax.ShapeDtypeStruct((depth, b, MEM_HEADS, m, MEM_HD), BF16)],
        compiler_params=pltpu.CompilerParams(dimension_semantics=("parallel", "parallel")),
        name="mem_kv",
    )(mem, mem_norm_g.reshape(depth, 1, d), w_mkv_bf, mem_k_g.reshape(depth, 1, MEM_HD))


def _pre_kernel(layer_ref, x_ref, ng_ref, w1_ref, qng_ref, kvng_ref, wuq_ref, wuk_ref, wuv_ref,
                qg_ref, qoff_ref, gn_ref, ga_ref, gb_ref, cosT_ref, sinT_ref, cosF_ref, sinF_ref,
                qT_ref, k_ref, vT_ref):
    del layer_ref
    half = ROPE // 2
    qg = qg_ref[0]
    qoff = qoff_ref[0]
    gn = gn_ref[0]
    lane = lax.broadcasted_iota(jnp.int32, (1, HEAD_PAD), 1)
    one_lanes = ((lane >= QK_DIM) & (lane < QK_DIM + SHIFT_ROWS)).astype(F32)
    subs = [slice(i * TS_PRE_SUB, (i + 1) * TS_PRE_SUB) for i in range(x_ref.shape[1] // TS_PRE_SUB)]

    a_subs = []
    for r in subs:
        hb = (_rms(x_ref[0, r, :]) * ng_ref[0]).astype(BF16)
        a_subs.append(jnp.dot(hb, w1_ref[0], preferred_element_type=F32))

    for r, a in zip(subs, a_subs):
        qn = (_rms(a[:, :Q_RANK]) * qng_ref[0]).astype(BF16)
        kvn = (_rms(a[:, OFF_KVLAT:OFF_KPE]) * kvng_ref[0]).astype(BF16)
        pe = a[:, OFF_KPE:OFF_KPE + HEAD_PAD]
        pe_sw = a[:, OFF_KPE + HEAD_PAD:W1_COLS]
        qT = lax.dot_general(wuq_ref[0], qn, NT_DIMS, preferred_element_type=F32)
        knope = jnp.dot(kvn, wuk_ref[0], preferred_element_type=F32)
        vT = lax.dot_general(wuv_ref[0], kvn, NT_DIMS, preferred_element_type=F32)

        cT = cosT_ref[0, :, r]
        sT = sinT_ref[0, :, r]
        for h in range(N_HEADS):
            qh = qT[h * HEAD_PAD:(h + 1) * HEAD_PAD]
            ms = jnp.sum(qh * qh, axis=0, keepdims=True) * (1.0 / QK_DIM)
            qh = qh * lax.rsqrt(ms + EPS) * qg + qoff
            t1 = qh[NOPE:NOPE + half]
            t2 = qh[NOPE + half:QK_DIM]
            qh = jnp.concatenate([qh[:NOPE], t1 * cT - t2 * sT, t2 * cT + t1 * sT, qh[QK_DIM:]], axis=0)
            qT_ref[0, h, :, r] = qh.astype(BF16)

        u = pe * (ga_ref[0] * cosF_ref[0, r, :]) + pe_sw * (gb_ref[0] * sinF_ref[0, r, :])
        ss_pe = jnp.sum(pe * pe, axis=-1, keepdims=True)
        for h in range(N_HEADS):
            kn = knope[:, h * HEAD_PAD:(h + 1) * HEAD_PAD]
            ms = (jnp.sum(kn * kn, axis=-1, keepdims=True) + ss_pe) * (1.0 / QK_DIM)
            k_ref[0, h, r, :] = ((kn * gn + u) * lax.rsqrt(ms + EPS) + one_lanes).astype(BF16)

        vT_ref[0, :, :, r] = vT.reshape(N_HEADS, V_DIM, vT.shape[-1]).astype(BF16)


def _pre_call(layer, x, p, tabs):
    b, s, d = x.shape
    cosT, sinT, cosF, sinF = tabs
    half = ROPE // 2
    wspec = lambda shape: pl.BlockSpec((1,) + shape, lambda i, j, l: (l[0],) + (0,) * len(shape))
    grid_spec = pltpu.PrefetchScalarGridSpec(
        num_scalar_prefetch=1,
        grid=(b, s // TS_PRE),
        in_specs=[pl.BlockSpec((1, TS_PRE, d), lambda i, j, l: (i, j, 0)),
                  wspec((1, d)), wspec((d, W1_COLS)), wspec((1, Q_RANK)), wspec((1, KV_RANK)),
                  wspec((N_HEADS * HEAD_PAD, Q_RANK)), wspec((KV_RANK, N_HEADS * HEAD_PAD)),
                  wspec((N_HEADS * V_DIM, KV_RANK)),
                  wspec((HEAD_PAD, 1)), wspec((HEAD_PAD, 1)),
                  wspec((1, HEAD_PAD)), wspec((1, HEAD_PAD)), wspec((1, HEAD_PAD)),
                  pl.BlockSpec((1, half, TS_PRE), lambda i, j, l: (i, 0, j)),
                  pl.BlockSpec((1, half, TS_PRE), lambda i, j, l: (i, 0, j)),
                  pl.BlockSpec((1, TS_PRE, HEAD_PAD), lambda i, j, l: (i, j, 0)),
                  pl.BlockSpec((1, TS_PRE, HEAD_PAD), lambda i, j, l: (i, j, 0))],
        out_specs=[pl.BlockSpec((1, N_HEADS, HEAD_PAD, TS_PRE), lambda i, j, l: (i, 0, 0, j)),
                   pl.BlockSpec((1, N_HEADS, TS_PRE, HEAD_PAD), lambda i, j, l: (i, 0, j, 0)),
                   pl.BlockSpec((1, N_HEADS, V_DIM, TS_PRE), lambda i, j, l: (i, 0, 0, j))],
    )
    return pl.pallas_call(
        _pre_kernel,
        grid_spec=grid_spec,
        out_shape=[jax.ShapeDtypeStruct((b, N_HEADS, HEAD_PAD, s), BF16),
                   jax.ShapeDtypeStruct((b, N_HEADS, s, HEAD_PAD), BF16),
                   jax.ShapeDtypeStruct((b, N_HEADS, V_DIM, s), BF16)],
        compiler_params=pltpu.CompilerParams(dimension_semantics=("parallel", "parallel"),
                                             vmem_limit_bytes=VMEM_LIMIT),
        name="pre_attn",
    )(layer, x, p["norm_g"], p["w1"], p["q_norm_g"], p["kv_norm_g"], p["w_uqT"], p["w_uk"], p["w_uvT"],
      p["qg"], p["qoff"], p["gn"], p["ga"], p["gb"], cosT, sinT, cosF, sinF)


def _rowmax_kernel(qT_ref, k_ref, m_ref):
    n_chunks = k_ref.shape[2] // TKB
    n_tiles = qT_ref.shape[3] // TQH

    def query_tile(j, carry):
        q0 = pl.multiple_of(j * TQH, TQH)
        qT = qT_ref[0, 0, :, pl.ds(q0, TQH)]
        m8 = jnp.full((SUBLANES, TQH), -jnp.inf, F32)
        for c in range(n_chunks):
            s = jnp.dot(k_ref[0, 0, c * TKB:(c + 1) * TKB, :], qT, preferred_element_type=F32)
            m8 = jnp.maximum(m8, jnp.max(s.reshape(TKB // SUBLANES, SUBLANES, TQH), axis=0))
        m_ref[0, 0, :, pl.ds(q0, TQH)] = jnp.max(m8, axis=0, keepdims=True)
        return carry

    lax.fori_loop(0, n_tiles, query_tile, 0)


def _rowmax_call(qT, k):
    b, nh, _, s = qT.shape
    assert s % TKB == 0 and s % TQH == 0
    return pl.pallas_call(
        _rowmax_kernel,
        grid=(b, nh),
        in_specs=[pl.BlockSpec((1, 1, HEAD_PAD, s), lambda i, h: (i, h, 0, 0)),
                  pl.BlockSpec((1, 1, s, HEAD_PAD), lambda i, h: (i, h, 0, 0))],
        out_specs=pl.BlockSpec((1, 1, 1, s), lambda i, h: (i, h, 0, 0)),
        out_shape=jax.ShapeDtypeStruct((b, nh, 1, s), F32),
        compiler_params=pltpu.CompilerParams(dimension_semantics=("parallel", "parallel"),
                                             vmem_limit_bytes=VMEM_LIMIT),
        name="mla_rowmax",
    )(qT, k)


def _attn_bounded_kernel(qT_ref, k_ref, vT_ref, oT_ref):
    n_chunks = k_ref.shape[2] // TKB
    n_tiles = qT_ref.shape[3] // TQB

    def query_tile(j, carry):
        q0 = pl.multiple_of(j * TQB, TQB)
        qT = qT_ref[0, 0, :, pl.ds(q0, TQB)]
        halves = [qT[:, h * TQH:(h + 1) * TQH] for h in range(TQB // TQH)]

        def scores(c, h):
            return jnp.dot(k_ref[0, 0, c * TKB:(c + 1) * TKB, :], halves[h], preferred_element_type=F32)

        n_h = len(halves)
        l8 = [jnp.zeros((SUBLANES, TQH), F32) for _ in range(n_h)]
        acc = [jnp.zeros((V_DIM, TQH), F32) for _ in range(n_h)]
        s_next = [scores(0, h) for h in range(n_h)]
        for c in range(n_chunks):
            vc = vT_ref[0, 0, :, c * TKB:(c + 1) * TKB]
            for h in range(n_h):
                s_cur = s_next[h]
                if c + 1 < n_chunks:
                    s_next[h] = scores(c + 1, h)
                p = jnp.exp2(s_cur)
                l8[h] = l8[h] + jnp.sum(p.reshape(TKB // SUBLANES, SUBLANES, TQH), axis=0)
                acc[h] = acc[h] + jnp.dot(vc, p.astype(BF16), preferred_element_type=F32)
        for h in range(n_h):
            oT_ref[0, 0, :, pl.ds(q0 + h * TQH, TQH)] = acc[h] / jnp.sum(l8[h], axis=0, keepdims=True)
        return carry

    lax.fori_loop(0, n_tiles, query_tile, 0)


def _attn_bounded_call(qT, k, vT):
    b, nh, _, s = qT.shape
    assert s % TKB == 0 and s % TQB == 0
    return pl.pallas_call(
        _attn_bounded_kernel,
        grid=(b, nh),
        in_specs=[pl.BlockSpec((1, 1, HEAD_PAD, s), lambda i, h: (i, h, 0, 0)),
                  pl.BlockSpec((1, 1, s, HEAD_PAD), lambda i, h: (i, h, 0, 0)),
                  pl.BlockSpec((1, 1, V_DIM, s), lambda i, h: (i, h, 0, 0))],
        out_specs=pl.BlockSpec((1, 1, V_DIM, s), lambda i, h: (i, h, 0, 0)),
        out_shape=jax.ShapeDtypeStruct((b, nh, V_DIM, s), F32),
        compiler_params=pltpu.CompilerParams(dimension_semantics=("parallel", "parallel"),
                                             vmem_limit_bytes=VMEM_LIMIT),
        name="mla_attn_bounded",
    )(qT, k, vT)


def _attn_rowmax_call(qT, k, vT):
    m = _rowmax_call(qT, k)
    hi = (-m).astype(BF16)
    lo = (-m - hi.astype(F32)).astype(BF16)
    qT = lax.dynamic_update_slice(qT, jnp.concatenate([hi, lo], axis=2), (0, 0, QK_DIM, 0))
    return _attn_bounded_call(qT, k, vT)


def _post_kernel(layer_ref, x_ref, xp_ref, xn_ref, oT_ref, mk_ref, mv_ref, ng_ref, wr_ref, bg_ref,
                 cw_ref, cb_ref, mqg_ref, wa_ref, wc_ref, wm_ref, wo_ref, out_ref):
    del layer_ref
    j = pl.program_id(1)
    nj = pl.num_programs(1)
    x = x_ref[0]
    t = x.shape[0]
    ng = ng_ref[0]
    hb = (_rms(x) * ng).astype(BF16)
    xh = jnp.concatenate([xp_ref[0, 0], xn_ref[0, 0]], axis=0)
    hh = (_rms(xh) * ng).astype(BF16)
    h_ext = jnp.concatenate([hb, hh], axis=0)

    def proj(lhs, lo, hi):
        return jnp.dot(lhs, wr_ref[0, :, lo:hi], preferred_element_type=F32)

    ccu = proj(h_ext, R_CC, R_QM)
    z_ext = ccu[:, :CONV_W] * ccu[:, CONV_W:]
    z = z_ext[:t]
    z_before = z_ext[t + HALO - 1:t + HALO] * (j > 0).astype(F32)
    z_after = z_ext[t + HALO:t + HALO + 1] * (j < nj - 1).astype(F32)
    row = lax.broadcasted_iota(jnp.int32, z.shape, 0)
    z_prev = jnp.where(row == 0, z_before, pltpu.roll(z, 1, 0))
    z_next = jnp.where(row == t - 1, z_after, pltpu.roll(z, t - 1, 0))
    cw = cw_ref[0]
    conv = z_prev * cw[0:1] + z * cw[1:2] + z_next * cw[2:3] + cb_ref[0]
    o_conv = proj(hb, R_CB, R_CC) * conv * _silu(proj(hb, R_GC, R_GM))
    y_conv = jnp.dot(o_conv.astype(BF16), wc_ref[0], preferred_element_type=F32)

    oT = oT_ref[0]
    o_attn = oT.reshape(MLA_W, t).T * _silu(proj(hb, R_GA, R_GC))
    y_attn = jnp.dot(o_attn.astype(BF16), wa_ref[0], preferred_element_type=F32)

    qm = proj(hb, R_QM, R_GA)
    mqg = mqg_ref[0]
    heads = []
    for h in range(MEM_HEADS):
        qh = (_rms(qm[:, h * MEM_HD:(h + 1) * MEM_HD]) * mqg).astype(BF16)
        s = lax.dot_general(qh, mk_ref[0, 0, h], NT_DIMS, preferred_element_type=F32)
        s = s * (MEM_HD ** -0.5)
        p = jnp.exp(s - jnp.max(s, axis=-1, keepdims=True))
        l = jnp.sum(p, axis=-1, keepdims=True)
        oh = jnp.dot(p.astype(BF16), mv_ref[0, 0, h], preferred_element_type=F32)
        heads.append(oh / l)
    o_mem = jnp.concatenate(heads, axis=-1) * _silu(proj(hb, R_GM, R_R))
    y_mem = jnp.dot(o_mem.astype(BF16), wm_ref[0], preferred_element_type=F32)

    bg = bg_ref[0]
    r_a = _sigmoid(proj(hb, R_R, R_R + D_MODEL) + bg[:, :D_MODEL])
    y = r_a * y_attn
    r_c = _sigmoid(proj(hb, R_R + D_MODEL, R_R + 2 * D_MODEL) + bg[:, D_MODEL:2 * D_MODEL])
    y = y + r_c * y_conv
    r_m = _sigmoid(proj(hb, R_R + 2 * D_MODEL, R_R + 3 * D_MODEL) + bg[:, 2 * D_MODEL:])
    y = y + r_m * y_mem
    out_ref[0] = x + jnp.dot(y.astype(BF16), wo_ref[0], preferred_element_type=F32)


def _post_call(layer, x, oT, mk, mv, p):
    b, s, d = x.shape
    m = mk.shape[3]
    nblk = TS_POST // HALO
    last = s // HALO - 1
    x4 = x.reshape(b, s // HALO, HALO, d)
    wspec = lambda shape: pl.BlockSpec((1,) + shape, lambda i, j, l: (l[0],) + (0,) * len(shape),
                                       pipeline_mode=pl.Buffered(1))
    grid_spec = pltpu.PrefetchScalarGridSpec(
        num_scalar_prefetch=1,
        grid=(b, s // TS_POST),
        in_specs=[pl.BlockSpec((1, TS_POST, d), lambda i, j, l: (i, j, 0)),
                  pl.BlockSpec((1, 1, HALO, d), lambda i, j, l: (i, jnp.maximum(j * nblk - 1, 0), 0, 0)),
                  pl.BlockSpec((1, 1, HALO, d), lambda i, j, l: (i, jnp.minimum((j + 1) * nblk, last), 0, 0)),
                  pl.BlockSpec((1, N_HEADS, V_DIM, TS_POST), lambda i, j, l: (i, 0, 0, j)),
                  pl.BlockSpec((1, 1, MEM_HEADS, m, MEM_HD), lambda i, j, l: (l[0], i, 0, 0, 0)),
                  pl.BlockSpec((1, 1, MEM_HEADS, m, MEM_HD), lambda i, j, l: (l[0], i, 0, 0, 0)),
                  wspec((1, d)), wspec((d, REST_COLS)), wspec((1, N_BRANCH * d)),
                  wspec((3, CONV_W)), wspec((1, CONV_W)), wspec((1, MEM_HD)),
                  wspec((MLA_W, d)), wspec((CONV_W, d)), wspec((MEM_W, d)), wspec((d, d))],
        out_specs=pl.BlockSpec((1, TS_POST, d), lambda i, j, l: (i, j, 0)),
    )
    return pl.pallas_call(
        _post_kernel,
        grid_spec=grid_spec,
        out_shape=jax.ShapeDtypeStruct((b, s, d), F32),
        compiler_params=pltpu.CompilerParams(dimension_semantics=("parallel", "parallel"),
                                             vmem_limit_bytes=VMEM_LIMIT),
        name="post_attn",
    )(layer, x, x4, x4, oT, mk, mv, p["norm_g"], p["w_rest"], p["b_gate"], p["conv_w"], p["conv_b"],
      p["mem_q_g"], p["w_br_attn"], p["w_br_conv"], p["w_br_mem"], p["w_out"])


def _prepare_params(norm_g, w_in, b_gate, q_norm_g, w_uq, kv_norm_g, w_ukv, q_head_g, k_head_g,
                    conv_w, conv_b, mem_q_g, w_br_attn, w_br_conv, w_br_mem, w_out):
    depth = w_in.shape[0]
    d = D_MODEL
    half = ROPE // 2
    z = lambda n: jnp.zeros((depth, d, n), F32)
    t1 = w_in[:, :, OFF_KPE:OFF_KPE + half]
    t2 = w_in[:, :, OFF_KPE + half:OFF_REST]
    pad = HEAD_PAD - QK_DIM
    w1 = jnp.concatenate([w_in[:, :, :OFF_KPE], z(NOPE), t1, t2, z(pad), z(NOPE), t2, t1, z(pad)], axis=-1)

    w_uq4 = w_uq.reshape(depth, Q_RANK, N_HEADS, QK_DIM)
    w_uq_pad = jnp.pad(w_uq4, ((0, 0), (0, 0), (0, 0), (0, pad))).reshape(depth, Q_RANK, N_HEADS * HEAD_PAD)
    w_ukv4 = w_ukv.reshape(depth, KV_RANK, N_HEADS, NOPE + V_DIM)
    w_uk = jnp.pad(w_ukv4[..., :NOPE], ((0, 0), (0, 0), (0, 0), (0, HEAD_PAD - NOPE)))
    w_uv = w_ukv4[..., NOPE:].reshape(depth, KV_RANK, N_HEADS * V_DIM)

    zg = lambda n: jnp.zeros((depth, n), F32)
    q_scale = (QK_DIM ** -0.5) * LOG2E
    qg = jnp.concatenate([q_head_g * q_scale, zg(pad)], axis=-1)[..., None]
    score_bound = (QK_DIM ** 0.5) * LOG2E * jnp.max(jnp.abs(q_head_g), axis=-1) * jnp.max(jnp.abs(k_head_g), axis=-1)
    bounded = score_bound <= -1.0
    off = jnp.where(bounded, -score_bound, 0.0)
    qoff = jnp.concatenate([zg(QK_DIM), off[:, None], zg(pad - 1)], axis=-1)[..., None]
    g_nope, g1, g2 = k_head_g[:, :NOPE], k_head_g[:, NOPE:NOPE + half], k_head_g[:, NOPE + half:]
    gn = jnp.concatenate([g_nope, zg(HEAD_PAD - NOPE)], axis=-1)[:, None, :]
    ga = jnp.concatenate([zg(NOPE), g1, g2, zg(pad)], axis=-1)[:, None, :]
    gb = jnp.concatenate([zg(NOPE), g2, g1, zg(pad)], axis=-1)[:, None, :]
    return dict(
        norm_g=norm_g[:, None, :],
        w1=w1.astype(BF16),
        q_norm_g=q_norm_g[:, None, :],
        kv_norm_g=kv_norm_g[:, None, :],
        w_uqT=jnp.swapaxes(w_uq_pad, 1, 2).astype(BF16),
        w_uk=w_uk.reshape(depth, KV_RANK, N_HEADS * HEAD_PAD).astype(BF16),
        w_uvT=jnp.swapaxes(w_uv, 1, 2).astype(BF16),
        qg=qg, qoff=qoff, bounded=bounded, gn=gn, ga=ga, gb=gb,
        w_rest=w_in[:, :, OFF_REST:].astype(BF16),
        b_gate=b_gate[:, None, :],
        conv_w=conv_w,
        conv_b=conv_b[:, None, :],
        mem_q_g=mem_q_g[:, None, :],
        w_br_attn=w_br_attn.astype(BF16),
        w_br_conv=w_br_conv.astype(BF16),
        w_br_mem=w_br_mem.astype(BF16),
        w_out=w_out.astype(BF16),
    )


def kernel(x, mem, positions, norm_g, w_in, b_gate, q_norm_g, w_uq, kv_norm_g, w_ukv, q_head_g, k_head_g,
           conv_w, conv_b, mem_norm_g, w_mkv, mem_q_g, mem_k_g, w_br_attn, w_br_conv, w_br_mem, w_out):
    depth = w_in.shape[0]
    assert x.shape[-1] == D_MODEL and w_in.shape[-1] == OFF_REST + REST_COLS
    assert x.shape[1] % max(TS_PRE, TS_POST, TQB, TS_TAB) == 0
    p = _prepare_params(norm_g, w_in, b_gate, q_norm_g, w_uq, kv_norm_g, w_ukv, q_head_g, k_head_g,
                        conv_w, conv_b, mem_q_g, w_br_attn, w_br_conv, w_br_mem, w_out)
    invf = ROPE_BASE ** (-jnp.arange(0, ROPE, 2, dtype=F32) / ROPE)
    tabs = _rope_tables(positions, invf[:, None])
    mk, mv = _mem_kv(mem, mem_norm_g, w_mkv.astype(BF16), mem_k_g)

    for i in range(depth):
        layer = jnp.full((1,), i, jnp.int32)
        qT, k, vT = _pre_call(layer, x, p, tabs)
        oT = lax.cond(p["bounded"][i], _attn_bounded_call, _attn_rowmax_call, qT, k, vT)
        x = _post_call(layer, x, oT, mk, mv, p)
    return x
```

```python
import math

import jax
import jax.numpy as jnp
from jax import lax
from jax.experimental import pallas as pl
from jax.experimental.pallas import tpu as pltpu

D_MODEL = 1024
N_HEADS = 8
NOPE = 64
ROPE = 32
QK_DIM = NOPE + ROPE
V_DIM = 64
Q_RANK = 3 * D_MODEL // 8
KV_RANK = D_MODEL // 4
MLA_W = N_HEADS * V_DIM
CONV_W = D_MODEL // 2
MEM_HEADS = 4
MEM_HD = 128
MEM_W = MEM_HEADS * MEM_HD
N_BRANCH = 3
ROPE_BASE = 10000.0
EPS = 1e-6
LOG2E = math.log2(math.e)

LANES = 128
SUBLANES = 8
HEAD_PAD = LANES
VMEM_LIMIT = 56 * 1024 * 1024

OFF_KVLAT = Q_RANK
OFF_KPE = Q_RANK + KV_RANK
OFF_REST = OFF_KPE + ROPE
W1_COLS = OFF_KPE + 2 * HEAD_PAD
R_CB, R_CC, R_CU, R_QM, R_GA, R_GC, R_GM, R_R = (0, 512, 1024, 1536, 2048, 2560, 3072, 3584)
REST_COLS = R_R + N_BRANCH * D_MODEL

TS_TAB = 512
TS_PRE = 1024
TS_PRE_SUB = 256
TS_POST = 512
HALO = SUBLANES
TQB = 1024
TQH = 512
TKB = 256
SHIFT_ROWS = 2
MAX_SCORE_BOUND = 48.0

F32 = jnp.float32
BF16 = jnp.bfloat16
NT_DIMS = (((1,), (1,)), ((), ()))


def _rms(t):
    return t * lax.rsqrt(jnp.mean(t * t, axis=-1, keepdims=True) + EPS)


def _sigmoid(t):
    return 1.0 / (1.0 + jnp.exp(-t))


def _silu(t):
    return t * _sigmoid(t)


def _tables_kernel(pos_ref, invf_ref, cosT_ref, sinT_ref, cosF_ref, sinF_ref):
    pos = pos_ref[0].astype(F32)
    ang = invf_ref[...] * pos
    c = jnp.cos(ang)
    s = jnp.sin(ang)
    cosT_ref[0] = c
    sinT_ref[0] = s
    t = pos.shape[1]
    z64 = jnp.zeros((NOPE, t), F32)
    z32 = jnp.zeros((HEAD_PAD - QK_DIM, t), F32)
    cosF_ref[0] = jnp.concatenate([z64, c, c, z32], axis=0).T
    sinF_ref[0] = jnp.concatenate([z64, -s, s, z32], axis=0).T


def _rope_tables(positions, invf):
    b, s = positions.shape
    half = ROPE // 2
    return pl.pallas_call(
        _tables_kernel,
        grid=(b, s // TS_TAB),
        in_specs=[pl.BlockSpec((1, 1, TS_TAB), lambda i, j: (i, 0, j)),
                  pl.BlockSpec((half, 1), lambda i, j: (0, 0))],
        out_specs=[pl.BlockSpec((1, half, TS_TAB), lambda i, j: (i, 0, j)),
                   pl.BlockSpec((1, half, TS_TAB), lambda i, j: (i, 0, j)),
                   pl.BlockSpec((1, TS_TAB, HEAD_PAD), lambda i, j: (i, j, 0)),
                   pl.BlockSpec((1, TS_TAB, HEAD_PAD), lambda i, j: (i, j, 0))],
        out_shape=[jax.ShapeDtypeStruct((b, half, s), F32),
                   jax.ShapeDtypeStruct((b, half, s), F32),
                   jax.ShapeDtypeStruct((b, s, HEAD_PAD), F32),
                   jax.ShapeDtypeStruct((b, s, HEAD_PAD), F32)],
        compiler_params=pltpu.CompilerParams(dimension_semantics=("parallel", "parallel")),
        name="rope_tables",
    )(positions.reshape(b, 1, s), invf)


def _mem_kernel(mem_ref, g_ref, w_ref, kg_ref, mk_ref, mv_ref):
    m = mem_ref[0]
    mn = (_rms(m) * g_ref[0]).astype(BF16)
    mkv = jnp.dot(mn, w_ref[0], preferred_element_type=F32)
    for h in range(MEM_HEADS):
        kh = mkv[:, h * 2 * MEM_HD: h * 2 * MEM_HD + MEM_HD]
        vh = mkv[:, h * 2 * MEM_HD + MEM_HD: (h + 1) * 2 * MEM_HD]
        mk_ref[0, 0, h] = (_rms(kh) * kg_ref[0]).astype(BF16)
        mv_ref[0, 0, h] = vh.astype(BF16)


def _mem_kv(mem, mem_norm_g, w_mkv_bf, mem_k_g):
    depth = w_mkv_bf.shape[0]
    b, m, d = mem.shape
    return pl.pallas_call(
        _mem_kernel,
        grid=(depth, b),
        in_specs=[pl.BlockSpec((1, m, d), lambda l, i: (i, 0, 0)),
                  pl.BlockSpec((1, 1, d), lambda l, i: (l, 0, 0)),
                  pl.BlockSpec((1, d, 2 * MEM_W), lambda l, i: (l, 0, 0)),
                  pl.BlockSpec((1, 1, MEM_HD), lambda l, i: (l, 0, 0))],
        out_specs=[pl.BlockSpec((1, 1, MEM_HEADS, m, MEM_HD), lambda l, i: (l, i, 0, 0, 0)),
                   pl.BlockSpec((1, 1, MEM_HEADS, m, MEM_HD), lambda l, i: (l, i, 0, 0, 0))],
        out_shape=[jax.ShapeDtypeStruct((depth, b, MEM_HEADS, m, MEM_HD), BF16),
                   jax.ShapeDtypeStruct((depth, b, MEM_HEADS, m, MEM_HD), BF16)],
        compiler_params=pltpu.CompilerParams(dimension_semantics=("parallel", "parallel")),
        name="mem_kv",
    )(mem, mem_norm_g.reshape(depth, 1, d), w_mkv_bf, mem_k_g.reshape(depth, 1, MEM_HD))


def _sub_blocks(n_rows):
    return [slice(i * TS_PRE_SUB, (i + 1) * TS_PRE_SUB) for i in range(n_rows // TS_PRE_SUB)]


def _pre_block(x_subs, ng_ref, w1_ref, qng_ref, kvng_ref, wuq_ref, wuk_ref, wuv_ref,
               qg_ref, qoff_ref, gn_ref, ga_ref, gb_ref, cosT_ref, sinT_ref, cosF_ref, sinF_ref,
               qT_ref, k_ref, vT_ref):
    half = ROPE // 2
    qg = qg_ref[0]
    qoff = qoff_ref[0]
    gn = gn_ref[0]
    lane = lax.broadcasted_iota(jnp.int32, (1, HEAD_PAD), 1)
    one_lanes = ((lane >= QK_DIM) & (lane < QK_DIM + SHIFT_ROWS)).astype(F32)

    a_subs = []
    for _, xs in x_subs:
        hb = (_rms(xs) * ng_ref[0]).astype(BF16)
        a_subs.append(jnp.dot(hb, w1_ref[0], preferred_element_type=F32))

    for (r, _), a in zip(x_subs, a_subs):
        qn = (_rms(a[:, :Q_RANK]) * qng_ref[0]).astype(BF16)
        kvn = (_rms(a[:, OFF_KVLAT:OFF_KPE]) * kvng_ref[0]).astype(BF16)
        pe = a[:, OFF_KPE:OFF_KPE + HEAD_PAD]
        pe_sw = a[:, OFF_KPE + HEAD_PAD:W1_COLS]
        qT = lax.dot_general(wuq_ref[0], qn, NT_DIMS, preferred_element_type=F32)
        knope = jnp.dot(kvn, wuk_ref[0], preferred_element_type=F32)
        vT = lax.dot_general(wuv_ref[0], kvn, NT_DIMS, preferred_element_type=F32)

        cT = cosT_ref[0, :, r]
        sT = sinT_ref[0, :, r]
        for h in range(N_HEADS):
            qh = qT[h * HEAD_PAD:(h + 1) * HEAD_PAD]
            ms = jnp.sum(qh * qh, axis=0, keepdims=True) * (1.0 / QK_DIM)
            qh = qh * lax.rsqrt(ms + EPS) * qg + qoff
            t1 = qh[NOPE:NOPE + half]
            t2 = qh[NOPE + half:QK_DIM]
            qh = jnp.concatenate([qh[:NOPE], t1 * cT - t2 * sT, t2 * cT + t1 * sT, qh[QK_DIM:]], axis=0)
            qT_ref[0, h, :, r] = qh.astype(BF16)

        u = pe * (ga_ref[0] * cosF_ref[0, r, :]) + pe_sw * (gb_ref[0] * sinF_ref[0, r, :])
        ss_pe = jnp.sum(pe * pe, axis=-1, keepdims=True)
        for h in range(N_HEADS):
            kn = knope[:, h * HEAD_PAD:(h + 1) * HEAD_PAD]
            ms = (jnp.sum(kn * kn, axis=-1, keepdims=True) + ss_pe) * (1.0 / QK_DIM)
            k_ref[0, h, r, :] = ((kn * gn + u) * lax.rsqrt(ms + EPS) + one_lanes).astype(BF16)

        vT_ref[0, :, :, r] = vT.reshape(N_HEADS, V_DIM, vT.shape[-1]).astype(BF16)


def _pre_kernel(layer_ref, x_ref, *refs):
    del layer_ref
    _pre_block([(r, x_ref[0, r, :]) for r in _sub_blocks(x_ref.shape[1])], *refs)


def _pre_call(layer, x, p, tabs):
    b, s, d = x.shape
    cosT, sinT, cosF, sinF = tabs
    half = ROPE // 2
    wspec = lambda shape: pl.BlockSpec((1,) + shape, lambda i, j, l: (l[0],) + (0,) * len(shape))
    grid_spec = pltpu.PrefetchScalarGridSpec(
        num_scalar_prefetch=1,
        grid=(b, s // TS_PRE),
        in_specs=[pl.BlockSpec((1, TS_PRE, d), lambda i, j, l: (i, j, 0)),
                  wspec((1, d)), wspec((d, W1_COLS)), wspec((1, Q_RANK)), wspec((1, KV_RANK)),
                  wspec((N_HEADS * HEAD_PAD, Q_RANK)), wspec((KV_RANK, N_HEADS * HEAD_PAD)),
                  wspec((N_HEADS * V_DIM, KV_RANK)),
                  wspec((HEAD_PAD, 1)), wspec((HEAD_PAD, 1)),
                  wspec((1, HEAD_PAD)), wspec((1, HEAD_PAD)), wspec((1, HEAD_PAD)),
                  pl.BlockSpec((1, half, TS_PRE), lambda i, j, l: (i, 0, j)),
                  pl.BlockSpec((1, half, TS_PRE), lambda i, j, l: (i, 0, j)),
                  pl.BlockSpec((1, TS_PRE, HEAD_PAD), lambda i, j, l: (i, j, 0)),
                  pl.BlockSpec((1, TS_PRE, HEAD_PAD), lambda i, j, l: (i, j, 0))],
        out_specs=[pl.BlockSpec((1, N_HEADS, HEAD_PAD, TS_PRE), lambda i, j, l: (i, 0, 0, j)),
                   pl.BlockSpec((1, N_HEADS, TS_PRE, HEAD_PAD), lambda i, j, l: (i, 0, j, 0)),
                   pl.BlockSpec((1, N_HEADS, V_DIM, TS_PRE), lambda i, j, l: (i, 0, 0, j))],
    )
    return pl.pallas_call(
        _pre_kernel,
        grid_spec=grid_spec,
        out_shape=[jax.ShapeDtypeStruct((b, N_HEADS, HEAD_PAD, s), BF16),
                   jax.ShapeDtypeStruct((b, N_HEADS, s, HEAD_PAD), BF16),
                   jax.ShapeDtypeStruct((b, N_HEADS, V_DIM, s), BF16)],
        compiler_params=pltpu.CompilerParams(dimension_semantics=("parallel", "parallel"),
                                             vmem_limit_bytes=VMEM_LIMIT),
        name="pre_attn",
    )(layer, x, p["norm_g"], p["w1"], p["q_norm_g"], p["kv_norm_g"], p["w_uqT"], p["w_uk"], p["w_uvT"],
      p["qg"], p["qoff"], p["gn"], p["ga"], p["gb"], cosT, sinT, cosF, sinF)


def _rowmax_kernel(qT_ref, k_ref, m_ref):
    n_chunks = k_ref.shape[2] // TKB
    n_tiles = qT_ref.shape[3] // TQH

    def query_tile(j, carry):
        q0 = pl.multiple_of(j * TQH, TQH)
        qT = qT_ref[0, 0, :, pl.ds(q0, TQH)]
        m8 = jnp.full((SUBLANES, TQH), -jnp.inf, F32)
        for c in range(n_chunks):
            s = jnp.dot(k_ref[0, 0, c * TKB:(c + 1) * TKB, :], qT, preferred_element_type=F32)
            m8 = jnp.maximum(m8, jnp.max(s.reshape(TKB // SUBLANES, SUBLANES, TQH), axis=0))
        m_ref[0, 0, :, pl.ds(q0, TQH)] = jnp.max(m8, axis=0, keepdims=True)
        return carry

    lax.fori_loop(0, n_tiles, query_tile, 0)


def _rowmax_call(qT, k):
    b, nh, _, s = qT.shape
    assert s % TKB == 0 and s % TQH == 0
    return pl.pallas_call(
        _rowmax_kernel,
        grid=(b, nh),
        in_specs=[pl.BlockSpec((1, 1, HEAD_PAD, s), lambda i, h: (i, h, 0, 0)),
                  pl.BlockSpec((1, 1, s, HEAD_PAD), lambda i, h: (i, h, 0, 0))],
        out_specs=pl.BlockSpec((1, 1, 1, s), lambda i, h: (i, h, 0, 0)),
        out_shape=jax.ShapeDtypeStruct((b, nh, 1, s), F32),
        compiler_params=pltpu.CompilerParams(dimension_semantics=("parallel", "parallel"),
                                             vmem_limit_bytes=VMEM_LIMIT),
        name="mla_rowmax",
    )(qT, k)


def _attn_bounded_kernel(qT_ref, k_ref, vT_ref, oT_ref):
    n_chunks = k_ref.shape[2] // TKB
    n_tiles = qT_ref.shape[3] // TQB

    def query_tile(j, carry):
        q0 = pl.multiple_of(j * TQB, TQB)
        qT = qT_ref[0, 0, :, pl.ds(q0, TQB)]
        halves = [qT[:, h * TQH:(h + 1) * TQH] for h in range(TQB // TQH)]

        def scores(c, h):
            return jnp.dot(k_ref[0, 0, c * TKB:(c + 1) * TKB, :], halves[h], preferred_element_type=F32)

        n_h = len(halves)
        l8 = [jnp.zeros((SUBLANES, TQH), F32) for _ in range(n_h)]
        acc = [jnp.zeros((V_DIM, TQH), F32) for _ in range(n_h)]
        s_next = [scores(0, h) for h in range(n_h)]
        for c in range(n_chunks):
            vc = vT_ref[0, 0, :, c * TKB:(c + 1) * TKB]
            for h in range(n_h):
                s_cur = s_next[h]
                if c + 1 < n_chunks:
                    s_next[h] = scores(c + 1, h)
                p = jnp.exp2(s_cur)
                l8[h] = l8[h] + jnp.sum(p.reshape(TKB // SUBLANES, SUBLANES, TQH), axis=0)
                acc[h] = acc[h] + jnp.dot(vc, p.astype(BF16), preferred_element_type=F32)
        for h in range(n_h):
            oT_ref[0, 0, :, pl.ds(q0 + h * TQH, TQH)] = acc[h] / jnp.sum(l8[h], axis=0, keepdims=True)
        return carry

    lax.fori_loop(0, n_tiles, query_tile, 0)


def _attn_bounded_call(qT, k, vT):
    b, nh, _, s = qT.shape
    assert s % TKB == 0 and s % TQB == 0
    return pl.pallas_call(
        _attn_bounded_kernel,
        grid=(b, nh),
        in_specs=[pl.BlockSpec((1, 1, HEAD_PAD, s), lambda i, h: (i, h, 0, 0)),
                  pl.BlockSpec((1, 1, s, HEAD_PAD), lambda i, h: (i, h, 0, 0)),
                  pl.BlockSpec((1, 1, V_DIM, s), lambda i, h: (i, h, 0, 0))],
        out_specs=pl.BlockSpec((1, 1, V_DIM, s), lambda i, h: (i, h, 0, 0)),
        out_shape=jax.ShapeDtypeStruct((b, nh, V_DIM, s), F32),
        compiler_params=pltpu.CompilerParams(dimension_semantics=("parallel", "parallel"),
                                             vmem_limit_bytes=VMEM_LIMIT),
        name="mla_attn_bounded",
    )(qT, k, vT)


def _attn_rowmax_call(qT, k, vT):
    m = _rowmax_call(qT, k)
    hi = (-m).astype(BF16)
    lo = (-m - hi.astype(F32)).astype(BF16)
    qT = lax.dynamic_update_slice(qT, jnp.concatenate([hi, lo], axis=2), (0, 0, QK_DIM, 0))
    return _attn_bounded_call(qT, k, vT)


N_POST_IN = 16
N_PRE_IN = 16


def _post_kernel(layer_ref, *refs):
    del layer_ref
    out_ref = refs[N_POST_IN]
    out_ref[0] = _post_block(*refs[:N_POST_IN])


def _post_pre_kernel(layers_ref, *refs):
    del layers_ref
    pre_in = refs[N_POST_IN:N_POST_IN + N_PRE_IN]
    out_ref, qT_ref, k_ref, vT_ref = refs[N_POST_IN + N_PRE_IN:]
    x_new = _post_block(*refs[:N_POST_IN])
    out_ref[0] = x_new
    _pre_block([(r, x_new[r]) for r in _sub_blocks(x_new.shape[0])], *pre_in, qT_ref, k_ref, vT_ref)


def _post_block(x_ref, xp_ref, xn_ref, oT_ref, mk_ref, mv_ref, ng_ref, wr_ref, bg_ref,
                cw_ref, cb_ref, mqg_ref, wa_ref, wc_ref, wm_ref, wo_ref):
    j = pl.program_id(1)
    nj = pl.num_programs(1)
    x = x_ref[0]
    t = x.shape[0]
    ng = ng_ref[0]
    hb = (_rms(x) * ng).astype(BF16)
    xh = jnp.concatenate([xp_ref[0, 0], xn_ref[0, 0]], axis=0)
    hh = (_rms(xh) * ng).astype(BF16)
    h_ext = jnp.concatenate([hb, hh], axis=0)

    def proj(lhs, lo, hi):
        return jnp.dot(lhs, wr_ref[0, :, lo:hi], preferred_element_type=F32)

    ccu = proj(h_ext, R_CC, R_QM)
    z_ext = ccu[:, :CONV_W] * ccu[:, CONV_W:]
    z = z_ext[:t]
    z_before = z_ext[t + HALO - 1:t + HALO] * (j > 0).astype(F32)
    z_after = z_ext[t + HALO:t + HALO + 1] * (j < nj - 1).astype(F32)
    row = lax.broadcasted_iota(jnp.int32, z.shape, 0)
    z_prev = jnp.where(row == 0, z_before, pltpu.roll(z, 1, 0))
    z_next = jnp.where(row == t - 1, z_after, pltpu.roll(z, t - 1, 0))
    cw = cw_ref[0]
    conv = z_prev * cw[0:1] + z * cw[1:2] + z_next * cw[2:3] + cb_ref[0]
    o_conv = proj(hb, R_CB, R_CC) * conv * _silu(proj(hb, R_GC, R_GM))
    y_conv = jnp.dot(o_conv.astype(BF16), wc_ref[0], preferred_element_type=F32)

    oT = oT_ref[0]
    o_attn = oT.reshape(MLA_W, t).T * _silu(proj(hb, R_GA, R_GC))
    y_attn = jnp.dot(o_attn.astype(BF16), wa_ref[0], preferred_element_type=F32)

    qm = proj(hb, R_QM, R_GA)
    mqg = mqg_ref[0]
    heads = []
    for h in range(MEM_HEADS):
        qh = (_rms(qm[:, h * MEM_HD:(h + 1) * MEM_HD]) * mqg).astype(BF16)
        s = lax.dot_general(qh, mk_ref[0, 0, h], NT_DIMS, preferred_element_type=F32)
        s = s * (MEM_HD ** -0.5)
        p = jnp.exp(s - jnp.max(s, axis=-1, keepdims=True))
        l = jnp.sum(p, axis=-1, keepdims=True)
        oh = jnp.dot(p.astype(BF16), mv_ref[0, 0, h], preferred_element_type=F32)
        heads.append(oh / l)
    o_mem = jnp.concatenate(heads, axis=-1) * _silu(proj(hb, R_GM, R_R))
    y_mem = jnp.dot(o_mem.astype(BF16), wm_ref[0], preferred_element_type=F32)

    bg = bg_ref[0]
    r_a = _sigmoid(proj(hb, R_R, R_R + D_MODEL) + bg[:, :D_MODEL])
    y = r_a * y_attn
    r_c = _sigmoid(proj(hb, R_R + D_MODEL, R_R + 2 * D_MODEL) + bg[:, D_MODEL:2 * D_MODEL])
    y = y + r_c * y_conv
    r_m = _sigmoid(proj(hb, R_R + 2 * D_MODEL, R_R + 3 * D_MODEL) + bg[:, 2 * D_MODEL:])
    y = y + r_m * y_mem
    return x + jnp.dot(y.astype(BF16), wo_ref[0], preferred_element_type=F32)


def _post_call(layers, x, oT, mk, mv, p, tabs=None):
    b, s, d = x.shape
    m = mk.shape[3]
    nblk = TS_POST // HALO
    last = s // HALO - 1
    half = ROPE // 2
    x4 = x.reshape(b, s // HALO, HALO, d)

    def wspec(shape, which=0):
        return pl.BlockSpec((1,) + shape, lambda i, j, l: (l[which],) + (0,) * len(shape),
                            pipeline_mode=pl.Buffered(1))

    in_specs = [pl.BlockSpec((1, TS_POST, d), lambda i, j, l: (i, j, 0)),
                pl.BlockSpec((1, 1, HALO, d), lambda i, j, l: (i, jnp.maximum(j * nblk - 1, 0), 0, 0)),
                pl.BlockSpec((1, 1, HALO, d), lambda i, j, l: (i, jnp.minimum((j + 1) * nblk, last), 0, 0)),
                pl.BlockSpec((1, N_HEADS, V_DIM, TS_POST), lambda i, j, l: (i, 0, 0, j)),
                pl.BlockSpec((1, 1, MEM_HEADS, m, MEM_HD), lambda i, j, l: (l[0], i, 0, 0, 0)),
                pl.BlockSpec((1, 1, MEM_HEADS, m, MEM_HD), lambda i, j, l: (l[0], i, 0, 0, 0)),
                wspec((1, d)), wspec((d, REST_COLS)), wspec((1, N_BRANCH * d)),
                wspec((3, CONV_W)), wspec((1, CONV_W)), wspec((1, MEM_HD)),
                wspec((MLA_W, d)), wspec((CONV_W, d)), wspec((MEM_W, d)), wspec((d, d))]
    operands = [x, x4, x4, oT, mk, mv, p["norm_g"], p["w_rest"], p["b_gate"], p["conv_w"], p["conv_b"],
                p["mem_q_g"], p["w_br_attn"], p["w_br_conv"], p["w_br_mem"], p["w_out"]]
    out_specs = [pl.BlockSpec((1, TS_POST, d), lambda i, j, l: (i, j, 0))]
    out_shape = [jax.ShapeDtypeStruct((b, s, d), F32)]
    assert len(operands) == N_POST_IN
    if tabs is not None:
        in_specs += [wspec((1, d), 1), wspec((d, W1_COLS), 1), wspec((1, Q_RANK), 1), wspec((1, KV_RANK), 1),
                     wspec((N_HEADS * HEAD_PAD, Q_RANK), 1), wspec((KV_RANK, N_HEADS * HEAD_PAD), 1),
                     wspec((N_HEADS * V_DIM, KV_RANK), 1),
                     wspec((HEAD_PAD, 1), 1), wspec((HEAD_PAD, 1), 1),
                     wspec((1, HEAD_PAD), 1), wspec((1, HEAD_PAD), 1), wspec((1, HEAD_PAD), 1),
                     pl.BlockSpec((1, half, TS_POST), lambda i, j, l: (i, 0, j)),
                     pl.BlockSpec((1, half, TS_POST), lambda i, j, l: (i, 0, j)),
                     pl.BlockSpec((1, TS_POST, HEAD_PAD), lambda i, j, l: (i, j, 0)),
                     pl.BlockSpec((1, TS_POST, HEAD_PAD), lambda i, j, l: (i, j, 0))]
        operands += [p["norm_g"], p["w1"], p["q_norm_g"], p["kv_norm_g"], p["w_uqT"], p["w_uk"], p["w_uvT"],
                     p["qg"], p["qoff"], p["gn"], p["ga"], p["gb"], *tabs]
        assert len(operands) == N_POST_IN + N_PRE_IN
        out_specs += [pl.BlockSpec((1, N_HEADS, HEAD_PAD, TS_POST), lambda i, j, l: (i, 0, 0, j)),
                      pl.BlockSpec((1, N_HEADS, TS_POST, HEAD_PAD), lambda i, j, l: (i, 0, j, 0)),
                      pl.BlockSpec((1, N_HEADS, V_DIM, TS_POST), lambda i, j, l: (i, 0, 0, j))]
        out_shape += [jax.ShapeDtypeStruct((b, N_HEADS, HEAD_PAD, s), BF16),
                      jax.ShapeDtypeStruct((b, N_HEADS, s, HEAD_PAD), BF16),
                      jax.ShapeDtypeStruct((b, N_HEADS, V_DIM, s), BF16)]
    grid_spec = pltpu.PrefetchScalarGridSpec(num_scalar_prefetch=1, grid=(b, s // TS_POST),
                                             in_specs=in_specs, out_specs=out_specs)
    return pl.pallas_call(
        _post_kernel if tabs is None else _post_pre_kernel,
        grid_spec=grid_spec,
        out_shape=out_shape,
        compiler_params=pltpu.CompilerParams(dimension_semantics=("parallel", "parallel"),
                                             vmem_limit_bytes=VMEM_LIMIT),
        name="post_attn" if tabs is None else "post_pre_attn",
    )(layers, *operands)


def _prepare_params(norm_g, w_in, b_gate, q_norm_g, w_uq, kv_norm_g, w_ukv, q_head_g, k_head_g,
                    conv_w, conv_b, mem_q_g, w_br_attn, w_br_conv, w_br_mem, w_out):
    depth = w_in.shape[0]
    d = D_MODEL
    half = ROPE // 2
    z = lambda n: jnp.zeros((depth, d, n), F32)
    t1 = w_in[:, :, OFF_KPE:OFF_KPE + half]
    t2 = w_in[:, :, OFF_KPE + half:OFF_REST]
    pad = HEAD_PAD - QK_DIM
    w1 = jnp.concatenate([w_in[:, :, :OFF_KPE], z(NOPE), t1, t2, z(pad), z(NOPE), t2, t1, z(pad)], axis=-1)

    w_uq4 = w_uq.reshape(depth, Q_RANK, N_HEADS, QK_DIM)
    w_uq_pad = jnp.pad(w_uq4, ((0, 0), (0, 0), (0, 0), (0, pad))).reshape(depth, Q_RANK, N_HEADS * HEAD_PAD)
    w_ukv4 = w_ukv.reshape(depth, KV_RANK, N_HEADS, NOPE + V_DIM)
    w_uk = jnp.pad(w_ukv4[..., :NOPE], ((0, 0), (0, 0), (0, 0), (0, HEAD_PAD - NOPE)))
    w_uv = w_ukv4[..., NOPE:].reshape(depth, KV_RANK, N_HEADS * V_DIM)

    zg = lambda n: jnp.zeros((depth, n), F32)
    q_scale = (QK_DIM ** -0.5) * LOG2E
    qg = jnp.concatenate([q_head_g * q_scale, zg(pad)], axis=-1)[..., None]
    score_bound = (QK_DIM ** 0.5) * LOG2E * jnp.max(jnp.abs(q_head_g), axis=-1) * jnp.max(jnp.abs(k_head_g), axis=-1)
    bounded = score_bound <= MAX_SCORE_BOUND
    off = jnp.where(bounded, -score_bound, 0.0)
    qoff = jnp.concatenate([zg(QK_DIM), off[:, None], zg(pad - 1)], axis=-1)[..., None]
    g_nope, g1, g2 = k_head_g[:, :NOPE], k_head_g[:, NOPE:NOPE + half], k_head_g[:, NOPE + half:]
    gn = jnp.concatenate([g_nope, zg(HEAD_PAD - NOPE)], axis=-1)[:, None, :]
    ga = jnp.concatenate([zg(NOPE), g1, g2, zg(pad)], axis=-1)[:, None, :]
    gb = jnp.concatenate([zg(NOPE), g2, g1, zg(pad)], axis=-1)[:, None, :]
    return dict(
        norm_g=norm_g[:, None, :],
        w1=w1.astype(BF16),
        q_norm_g=q_norm_g[:, None, :],
        kv_norm_g=kv_norm_g[:, None, :],
        w_uqT=jnp.swapaxes(w_uq_pad, 1, 2).astype(BF16),
        w_uk=w_uk.reshape(depth, KV_RANK, N_HEADS * HEAD_PAD).astype(BF16),
        w_uvT=jnp.swapaxes(w_uv, 1, 2).astype(BF16),
        qg=qg, qoff=qoff, bounded=bounded, gn=gn, ga=ga, gb=gb,
        w_rest=w_in[:, :, OFF_REST:].astype(BF16),
        b_gate=b_gate[:, None, :],
        conv_w=conv_w,
        conv_b=conv_b[:, None, :],
        mem_q_g=mem_q_g[:, None, :],
        w_br_attn=w_br_attn.astype(BF16),
        w_br_conv=w_br_conv.astype(BF16),
        w_br_mem=w_br_mem.astype(BF16),
        w_out=w_out.astype(BF16),
    )


def kernel(x, mem, positions, norm_g, w_in, b_gate, q_norm_g, w_uq, kv_norm_g, w_ukv, q_head_g, k_head_g,
           conv_w, conv_b, mem_norm_g, w_mkv, mem_q_g, mem_k_g, w_br_attn, w_br_conv, w_br_mem, w_out):
    depth = w_in.shape[0]
    assert x.shape[-1] == D_MODEL and w_in.shape[-1] == OFF_REST + REST_COLS
    assert x.shape[1] % max(TS_PRE, TS_POST, TQB, TS_TAB) == 0
    p = _prepare_params(norm_g, w_in, b_gate, q_norm_g, w_uq, kv_norm_g, w_ukv, q_head_g, k_head_g,
                        conv_w, conv_b, mem_q_g, w_br_attn, w_br_conv, w_br_mem, w_out)
    invf = ROPE_BASE ** (-jnp.arange(0, ROPE, 2, dtype=F32) / ROPE)
    tabs = _rope_tables(positions, invf[:, None])
    mk, mv = _mem_kv(mem, mem_norm_g, w_mkv.astype(BF16), mem_k_g)

    qT, k, vT = _pre_call(jnp.zeros((1,), jnp.int32), x, p, tabs)
    for i in range(depth):
        oT = lax.cond(p["bounded"][i], _attn_bounded_call, _attn_rowmax_call, qT, k, vT)
        layers = jnp.array([i, i + 1], jnp.int32)
        if i + 1 < depth:
            x, qT, k, vT = _post_call(layers, x, oT, mk, mv, p, tabs)
        else:
            x, = _post_call(layers, x, oT, mk, mv, p)
    return x
```

```python
import math

import jax
import jax.numpy as jnp
from jax import lax
from jax.experimental import pallas as pl
from jax.experimental.pallas import tpu as pltpu

D_MODEL = 1024
N_HEADS = 8
NOPE = 64
ROPE = 32
QK_DIM = NOPE + ROPE
V_DIM = 64
Q_RANK = 3 * D_MODEL // 8
KV_RANK = D_MODEL // 4
MLA_W = N_HEADS * V_DIM
CONV_W = D_MODEL // 2
MEM_HEADS = 4
MEM_HD = 128
MEM_W = MEM_HEADS * MEM_HD
N_BRANCH = 3
ROPE_BASE = 10000.0
EPS = 1e-6
LOG2E = math.log2(math.e)

LANES = 128
SUBLANES = 8
HEAD_PAD = LANES
VMEM_LIMIT = 56 * 1024 * 1024

OFF_KVLAT = Q_RANK
OFF_KPE = Q_RANK + KV_RANK
OFF_REST = OFF_KPE + ROPE
W1_COLS = OFF_KPE + 2 * HEAD_PAD
R_CB, R_CC, R_CU, R_QM, R_GA, R_GC, R_GM, R_R = (0, 512, 1024, 1536, 2048, 2560, 3072, 3584)
REST_COLS = R_R + N_BRANCH * D_MODEL

TS_TAB = 512
TS_PRE = 1024
TS_PRE_SUB = 256
TS_POST = 512
HALO = SUBLANES
TQB = 1024
TQH = 512
TKB = 256
SHIFT_ROWS = 2
MAX_SCORE_BOUND = 48.0

F32 = jnp.float32
BF16 = jnp.bfloat16
NT_DIMS = (((1,), (1,)), ((), ()))


def _rms(t):
    return t * lax.rsqrt(jnp.mean(t * t, axis=-1, keepdims=True) + EPS)


def _sigmoid(t):
    return 0.5 * jnp.tanh(0.5 * t) + 0.5


def _silu(t):
    return t * _sigmoid(t)


def _tables_kernel(pos_ref, invf_ref, cosT_ref, sinT_ref, cosF_ref, sinF_ref):
    pos = pos_ref[0].astype(F32)
    ang = invf_ref[...] * pos
    c = jnp.cos(ang)
    s = jnp.sin(ang)
    cosT_ref[0] = c
    sinT_ref[0] = s
    t = pos.shape[1]
    z64 = jnp.zeros((NOPE, t), F32)
    z32 = jnp.zeros((HEAD_PAD - QK_DIM, t), F32)
    cosF_ref[0] = jnp.concatenate([z64, c, c, z32], axis=0).T
    sinF_ref[0] = jnp.concatenate([z64, -s, s, z32], axis=0).T


def _rope_tables(positions, invf):
    b, s = positions.shape
    half = ROPE // 2
    return pl.pallas_call(
        _tables_kernel,
        grid=(b, s // TS_TAB),
        in_specs=[pl.BlockSpec((1, 1, TS_TAB), lambda i, j: (i, 0, j)),
                  pl.BlockSpec((half, 1), lambda i, j: (0, 0))],
        out_specs=[pl.BlockSpec((1, half, TS_TAB), lambda i, j: (i, 0, j)),
                   pl.BlockSpec((1, half, TS_TAB), lambda i, j: (i, 0, j)),
                   pl.BlockSpec((1, TS_TAB, HEAD_PAD), lambda i, j: (i, j, 0)),
                   pl.BlockSpec((1, TS_TAB, HEAD_PAD), lambda i, j: (i, j, 0))],
        out_shape=[jax.ShapeDtypeStruct((b, half, s), F32),
                   jax.ShapeDtypeStruct((b, half, s), F32),
                   jax.ShapeDtypeStruct((b, s, HEAD_PAD), F32),
                   jax.ShapeDtypeStruct((b, s, HEAD_PAD), F32)],
        compiler_params=pltpu.CompilerParams(dimension_semantics=("parallel", "parallel")),
        name="rope_tables",
    )(positions.reshape(b, 1, s), invf)


def _mem_kernel(mem_ref, g_ref, w_ref, kg_ref, mk_ref, mv_ref):
    m = mem_ref[0]
    mn = (_rms(m) * g_ref[0]).astype(BF16)
    mkv = jnp.dot(mn, w_ref[0], preferred_element_type=F32)
    for h in range(MEM_HEADS):
        kh = mkv[:, h * 2 * MEM_HD: h * 2 * MEM_HD + MEM_HD]
        vh = mkv[:, h * 2 * MEM_HD + MEM_HD: (h + 1) * 2 * MEM_HD]
        mk_ref[0, 0, h] = (_rms(kh) * kg_ref[0]).astype(BF16)
        mv_ref[0, 0, h] = vh.astype(BF16)


def _mem_kv(mem, mem_norm_g, w_mkv_bf, mem_k_g):
    depth = w_mkv_bf.shape[0]
    b, m, d = mem.shape
    return pl.pallas_call(
        _mem_kernel,
        grid=(depth, b),
        in_specs=[pl.BlockSpec((1, m, d), lambda l, i: (i, 0, 0)),
                  pl.BlockSpec((1, 1, d), lambda l, i: (l, 0, 0)),
                  pl.BlockSpec((1, d, 2 * MEM_W), lambda l, i: (l, 0, 0)),
                  pl.BlockSpec((1, 1, MEM_HD), lambda l, i: (l, 0, 0))],
        out_specs=[pl.BlockSpec((1, 1, MEM_HEADS, m, MEM_HD), lambda l, i: (l, i, 0, 0, 0)),
                   pl.BlockSpec((1, 1, MEM_HEADS, m, MEM_HD), lambda l, i: (l, i, 0, 0, 0))],
        out_shape=[jax.ShapeDtypeStruct((depth, b, MEM_HEADS, m, MEM_HD), BF16),
                   jax.ShapeDtypeStruct((depth, b, MEM_HEADS, m, MEM_HD), BF16)],
        compiler_params=pltpu.CompilerParams(dimension_semantics=("parallel", "parallel")),
        name="mem_kv",
    )(mem, mem_norm_g.reshape(depth, 1, d), w_mkv_bf, mem_k_g.reshape(depth, 1, MEM_HD))


def _pre_kernel(layer_ref, x_ref, ng_ref, w1_ref, qng_ref, kvng_ref, wuq_ref, wuk_ref, wuv_ref,
                qg_ref, qoff_ref, gn_ref, ga_ref, gb_ref, cosT_ref, sinT_ref, cosF_ref, sinF_ref,
                qT_ref, k_ref, vT_ref):
    del layer_ref
    half = ROPE // 2
    qg = qg_ref[0]
    qoff = qoff_ref[0]
    gn = gn_ref[0]
    lane = lax.broadcasted_iota(jnp.int32, (1, HEAD_PAD), 1)
    one_lanes = ((lane >= QK_DIM) & (lane < QK_DIM + SHIFT_ROWS)).astype(F32)
    subs = [slice(i * TS_PRE_SUB, (i + 1) * TS_PRE_SUB) for i in range(x_ref.shape[1] // TS_PRE_SUB)]

    a_subs = []
    for r in subs:
        hb = (_rms(x_ref[0, r, :]) * ng_ref[0]).astype(BF16)
        a_subs.append(jnp.dot(hb, w1_ref[0], preferred_element_type=F32))

    for r, a in zip(subs, a_subs):
        qn = (_rms(a[:, :Q_RANK]) * qng_ref[0]).astype(BF16)
        kvn = (_rms(a[:, OFF_KVLAT:OFF_KPE]) * kvng_ref[0]).astype(BF16)
        pe = a[:, OFF_KPE:OFF_KPE + HEAD_PAD]
        pe_sw = a[:, OFF_KPE + HEAD_PAD:W1_COLS]
        qT = lax.dot_general(wuq_ref[0], qn, NT_DIMS, preferred_element_type=F32)
        knope = jnp.dot(kvn, wuk_ref[0], preferred_element_type=F32)
        vT = lax.dot_general(wuv_ref[0], kvn, NT_DIMS, preferred_element_type=F32)

        cT = cosT_ref[0, :, r]
        sT = sinT_ref[0, :, r]
        for h in range(N_HEADS):
            qh = qT[h * HEAD_PAD:(h + 1) * HEAD_PAD]
            ms = jnp.sum(qh * qh, axis=0, keepdims=True) * (1.0 / QK_DIM)
            qh = qh * lax.rsqrt(ms + EPS) * qg + qoff
            t1 = qh[NOPE:NOPE + half]
            t2 = qh[NOPE + half:QK_DIM]
            qh = jnp.concatenate([qh[:NOPE], t1 * cT - t2 * sT, t2 * cT + t1 * sT, qh[QK_DIM:]], axis=0)
            qT_ref[0, h, :, r] = qh.astype(BF16)

        u = pe * (ga_ref[0] * cosF_ref[0, r, :]) + pe_sw * (gb_ref[0] * sinF_ref[0, r, :])
        ss_pe = jnp.sum(pe * pe, axis=-1, keepdims=True)
        for h in range(N_HEADS):
            kn = knope[:, h * HEAD_PAD:(h + 1) * HEAD_PAD]
            ms = (jnp.sum(kn * kn, axis=-1, keepdims=True) + ss_pe) * (1.0 / QK_DIM)
            k_ref[0, h, r, :] = ((kn * gn + u) * lax.rsqrt(ms + EPS) + one_lanes).astype(BF16)

        vT_ref[0, :, :, r] = vT.reshape(N_HEADS, V_DIM, vT.shape[-1]).astype(BF16)


def _pre_call(layer, x, p, tabs):
    b, s, d = x.shape
    cosT, sinT, cosF, sinF = tabs
    half = ROPE // 2
    wspec = lambda shape: pl.BlockSpec((1,) + shape, lambda i, j, l: (l[0],) + (0,) * len(shape))
    grid_spec = pltpu.PrefetchScalarGridSpec(
        num_scalar_prefetch=1,
        grid=(b, s // TS_PRE),
        in_specs=[pl.BlockSpec((1, TS_PRE, d), lambda i, j, l: (i, j, 0)),
                  wspec((1, d)), wspec((d, W1_COLS)), wspec((1, Q_RANK)), wspec((1, KV_RANK)),
                  wspec((N_HEADS * HEAD_PAD, Q_RANK)), wspec((KV_RANK, N_HEADS * HEAD_PAD)),
                  wspec((N_HEADS * V_DIM, KV_RANK)),
                  wspec((HEAD_PAD, 1)), wspec((HEAD_PAD, 1)),
                  wspec((1, HEAD_PAD)), wspec((1, HEAD_PAD)), wspec((1, HEAD_PAD)),
                  pl.BlockSpec((1, half, TS_PRE), lambda i, j, l: (i, 0, j)),
                  pl.BlockSpec((1, half, TS_PRE), lambda i, j, l: (i, 0, j)),
                  pl.BlockSpec((1, TS_PRE, HEAD_PAD), lambda i, j, l: (i, j, 0)),
                  pl.BlockSpec((1, TS_PRE, HEAD_PAD), lambda i, j, l: (i, j, 0))],
        out_specs=[pl.BlockSpec((1, N_HEADS, HEAD_PAD, TS_PRE), lambda i, j, l: (i, 0, 0, j)),
                   pl.BlockSpec((1, N_HEADS, TS_PRE, HEAD_PAD), lambda i, j, l: (i, 0, j, 0)),
                   pl.BlockSpec((1, N_HEADS, V_DIM, TS_PRE), lambda i, j, l: (i, 0, 0, j))],
    )
    return pl.pallas_call(
        _pre_kernel,
        grid_spec=grid_spec,
        out_shape=[jax.ShapeDtypeStruct((b, N_HEADS, HEAD_PAD, s), BF16),
                   jax.ShapeDtypeStruct((b, N_HEADS, s, HEAD_PAD), BF16),
                   jax.ShapeDtypeStruct((b, N_HEADS, V_DIM, s), BF16)],
        compiler_params=pltpu.CompilerParams(dimension_semantics=("parallel", "parallel"),
                                             vmem_limit_bytes=VMEM_LIMIT),
        name="pre_attn",
    )(layer, x, p["norm_g"], p["w1"], p["q_norm_g"], p["kv_norm_g"], p["w_uqT"], p["w_uk"], p["w_uvT"],
      p["qg"], p["qoff"], p["gn"], p["ga"], p["gb"], cosT, sinT, cosF, sinF)


def _rowmax_kernel(qT_ref, k_ref, m_ref):
    n_chunks = k_ref.shape[2] // TKB
    n_tiles = qT_ref.shape[3] // TQH

    def query_tile(j, carry):
        q0 = pl.multiple_of(j * TQH, TQH)
        qT = qT_ref[0, 0, :, pl.ds(q0, TQH)]
        m8 = jnp.full((SUBLANES, TQH), -jnp.inf, F32)
        for c in range(n_chunks):
            s = jnp.dot(k_ref[0, 0, c * TKB:(c + 1) * TKB, :], qT, preferred_element_type=F32)
            m8 = jnp.maximum(m8, jnp.max(s.reshape(TKB // SUBLANES, SUBLANES, TQH), axis=0))
        m_ref[0, 0, :, pl.ds(q0, TQH)] = jnp.max(m8, axis=0, keepdims=True)
        return carry

    lax.fori_loop(0, n_tiles, query_tile, 0)


def _rowmax_call(qT, k):
    b, nh, _, s = qT.shape
    assert s % TKB == 0 and s % TQH == 0
    return pl.pallas_call(
        _rowmax_kernel,
        grid=(b, nh),
        in_specs=[pl.BlockSpec((1, 1, HEAD_PAD, s), lambda i, h: (i, h, 0, 0)),
                  pl.BlockSpec((1, 1, s, HEAD_PAD), lambda i, h: (i, h, 0, 0))],
        out_specs=pl.BlockSpec((1, 1, 1, s), lambda i, h: (i, h, 0, 0)),
        out_shape=jax.ShapeDtypeStruct((b, nh, 1, s), F32),
        compiler_params=pltpu.CompilerParams(dimension_semantics=("parallel", "parallel"),
                                             vmem_limit_bytes=VMEM_LIMIT),
        name="mla_rowmax",
    )(qT, k)


def _attn_bounded_kernel(qT_ref, k_ref, vT_ref, oT_ref):
    n_chunks = k_ref.shape[2] // TKB
    n_tiles = qT_ref.shape[3] // TQB

    def query_tile(j, carry):
        q0 = pl.multiple_of(j * TQB, TQB)
        qT = qT_ref[0, 0, :, pl.ds(q0, TQB)]
        halves = [qT[:, h * TQH:(h + 1) * TQH] for h in range(TQB // TQH)]

        def scores(c, h):
            return jnp.dot(k_ref[0, 0, c * TKB:(c + 1) * TKB, :], halves[h], preferred_element_type=F32)

        n_h = len(halves)
        l8 = [jnp.zeros((SUBLANES, TQH), F32) for _ in range(n_h)]
        acc = [jnp.zeros((V_DIM, TQH), F32) for _ in range(n_h)]
        s_next = [scores(0, h) for h in range(n_h)]
        for c in range(n_chunks):
            vc = vT_ref[0, 0, :, c * TKB:(c + 1) * TKB]
            for h in range(n_h):
                s_cur = s_next[h]
                if c + 1 < n_chunks:
                    s_next[h] = scores(c + 1, h)
                p = jnp.exp2(s_cur)
                l8[h] = l8[h] + jnp.sum(p.reshape(TKB // SUBLANES, SUBLANES, TQH), axis=0)
                acc[h] = acc[h] + jnp.dot(vc, p.astype(BF16), preferred_element_type=F32)
        for h in range(n_h):
            oT_ref[0, 0, :, pl.ds(q0 + h * TQH, TQH)] = acc[h] / jnp.sum(l8[h], axis=0, keepdims=True)
        return carry

    lax.fori_loop(0, n_tiles, query_tile, 0)


def _attn_bounded_call(qT, k, vT):
    b, nh, _, s = qT.shape
    assert s % TKB == 0 and s % TQB == 0
    return pl.pallas_call(
        _attn_bounded_kernel,
        grid=(b, nh),
        in_specs=[pl.BlockSpec((1, 1, HEAD_PAD, s), lambda i, h: (i, h, 0, 0)),
                  pl.BlockSpec((1, 1, s, HEAD_PAD), lambda i, h: (i, h, 0, 0)),
                  pl.BlockSpec((1, 1, V_DIM, s), lambda i, h: (i, h, 0, 0))],
        out_specs=pl.BlockSpec((1, 1, V_DIM, s), lambda i, h: (i, h, 0, 0)),
        out_shape=jax.ShapeDtypeStruct((b, nh, V_DIM, s), F32),
        compiler_params=pltpu.CompilerParams(dimension_semantics=("parallel", "parallel"),
                                             vmem_limit_bytes=VMEM_LIMIT),
        name="mla_attn_bounded",
    )(qT, k, vT)


def _attn_rowmax_call(qT, k, vT):
    m = _rowmax_call(qT, k)
    hi = (-m).astype(BF16)
    lo = (-m - hi.astype(F32)).astype(BF16)
    qT = lax.dynamic_update_slice(qT, jnp.concatenate([hi, lo], axis=2), (0, 0, QK_DIM, 0))
    return _attn_bounded_call(qT, k, vT)


def _post_kernel(layer_ref, x_ref, xp_ref, xn_ref, oT_ref, mk_ref, mv_ref, ng_ref, wr_ref, bg_ref,
                 cw_ref, cb_ref, mqg_ref, wa_ref, wc_ref, wm_ref, wo_ref, out_ref):
    del layer_ref
    j = pl.program_id(1)
    nj = pl.num_programs(1)
    x = x_ref[0]
    t = x.shape[0]
    ng = ng_ref[0]
    hb = (_rms(x) * ng).astype(BF16)
    xh = jnp.concatenate([xp_ref[0, 0], xn_ref[0, 0]], axis=0)
    hh = (_rms(xh) * ng).astype(BF16)
    h_ext = jnp.concatenate([hb, hh], axis=0)

    def proj(lhs, lo, hi):
        return jnp.dot(lhs, wr_ref[0, :, lo:hi], preferred_element_type=F32)

    ccu = proj(h_ext, R_CC, R_QM)
    z_ext = ccu[:, :CONV_W] * ccu[:, CONV_W:]
    z = z_ext[:t]
    z_before = z_ext[t + HALO - 1:t + HALO] * (j > 0).astype(F32)
    z_after = z_ext[t + HALO:t + HALO + 1] * (j < nj - 1).astype(F32)
    row = lax.broadcasted_iota(jnp.int32, z.shape, 0)
    z_prev = jnp.where(row == 0, z_before, pltpu.roll(z, 1, 0))
    z_next = jnp.where(row == t - 1, z_after, pltpu.roll(z, t - 1, 0))
    cw = cw_ref[0]
    conv = z_prev * cw[0:1] + z * cw[1:2] + z_next * cw[2:3] + cb_ref[0]
    o_conv = proj(hb, R_CB, R_CC) * conv * _silu(proj(hb, R_GC, R_GM))
    y_conv = jnp.dot(o_conv.astype(BF16), wc_ref[0], preferred_element_type=F32)

    oT = oT_ref[0]
    o_attn = oT.reshape(MLA_W, t).T * _silu(proj(hb, R_GA, R_GC))
    y_attn = jnp.dot(o_attn.astype(BF16), wa_ref[0], preferred_element_type=F32)

    qm = proj(hb, R_QM, R_GA)
    mqg = mqg_ref[0]
    heads = []
    for h in range(MEM_HEADS):
        qh = (_rms(qm[:, h * MEM_HD:(h + 1) * MEM_HD]) * mqg).astype(BF16)
        s = lax.dot_general(qh, mk_ref[0, 0, h], NT_DIMS, preferred_element_type=F32)
        s = s * (MEM_HD ** -0.5)
        p = jnp.exp(s - jnp.max(s, axis=-1, keepdims=True))
        l = jnp.sum(p, axis=-1, keepdims=True)
        oh = jnp.dot(p.astype(BF16), mv_ref[0, 0, h], preferred_element_type=F32)
        heads.append(oh / l)
    o_mem = jnp.concatenate(heads, axis=-1) * _silu(proj(hb, R_GM, R_R))
    y_mem = jnp.dot(o_mem.astype(BF16), wm_ref[0], preferred_element_type=F32)

    bg = bg_ref[0]
    r_a = _sigmoid(proj(hb, R_R, R_R + D_MODEL) + bg[:, :D_MODEL])
    y = r_a * y_attn
    r_c = _sigmoid(proj(hb, R_R + D_MODEL, R_R + 2 * D_MODEL) + bg[:, D_MODEL:2 * D_MODEL])
    y = y + r_c * y_conv
    r_m = _sigmoid(proj(hb, R_R + 2 * D_MODEL, R_R + 3 * D_MODEL) + bg[:, 2 * D_MODEL:])
    y = y + r_m * y_mem
    out_ref[0] = x + jnp.dot(y.astype(BF16), wo_ref[0], preferred_element_type=F32)


def _post_call(layer, x, oT, mk, mv, p):
    b, s, d = x.shape
    m = mk.shape[3]
    nblk = TS_POST // HALO
    last = s // HALO - 1
    x4 = x.reshape(b, s // HALO, HALO, d)
    wspec = lambda shape: pl.BlockSpec((1,) + shape, lambda i, j, l: (l[0],) + (0,) * len(shape),
                                       pipeline_mode=pl.Buffered(1))
    grid_spec = pltpu.PrefetchScalarGridSpec(
        num_scalar_prefetch=1,
        grid=(b, s // TS_POST),
        in_specs=[pl.BlockSpec((1, TS_POST, d), lambda i, j, l: (i, j, 0)),
                  pl.BlockSpec((1, 1, HALO, d), lambda i, j, l: (i, jnp.maximum(j * nblk - 1, 0), 0, 0)),
                  pl.BlockSpec((1, 1, HALO, d), lambda i, j, l: (i, jnp.minimum((j + 1) * nblk, last), 0, 0)),
                  pl.BlockSpec((1, N_HEADS, V_DIM, TS_POST), lambda i, j, l: (i, 0, 0, j)),
                  pl.BlockSpec((1, 1, MEM_HEADS, m, MEM_HD), lambda i, j, l: (l[0], i, 0, 0, 0)),
                  pl.BlockSpec((1, 1, MEM_HEADS, m, MEM_HD), lambda i, j, l: (l[0], i, 0, 0, 0)),
                  wspec((1, d)), wspec((d, REST_COLS)), wspec((1, N_BRANCH * d)),
                  wspec((3, CONV_W)), wspec((1, CONV_W)), wspec((1, MEM_HD)),
                  wspec((MLA_W, d)), wspec((CONV_W, d)), wspec((MEM_W, d)), wspec((d, d))],
        out_specs=pl.BlockSpec((1, TS_POST, d), lambda i, j, l: (i, j, 0)),
    )
    return pl.pallas_call(
        _post_kernel,
        grid_spec=grid_spec,
        out_shape=jax.ShapeDtypeStruct((b, s, d), F32),
        compiler_params=pltpu.CompilerParams(dimension_semantics=("parallel", "parallel"),
                                             vmem_limit_bytes=VMEM_LIMIT),
        name="post_attn",
    )(layer, x, x4, x4, oT, mk, mv, p["norm_g"], p["w_rest"], p["b_gate"], p["conv_w"], p["conv_b"],
      p["mem_q_g"], p["w_br_attn"], p["w_br_conv"], p["w_br_mem"], p["w_out"])


def _prepare_params(norm_g, w_in, b_gate, q_norm_g, w_uq, kv_norm_g, w_ukv, q_head_g, k_head_g,
                    conv_w, conv_b, mem_q_g, w_br_attn, w_br_conv, w_br_mem, w_out):
    depth = w_in.shape[0]
    d = D_MODEL
    half = ROPE // 2
    z = lambda n: jnp.zeros((depth, d, n), F32)
    t1 = w_in[:, :, OFF_KPE:OFF_KPE + half]
    t2 = w_in[:, :, OFF_KPE + half:OFF_REST]
    pad = HEAD_PAD - QK_DIM
    w1 = jnp.concatenate([w_in[:, :, :OFF_KPE], z(NOPE), t1, t2, z(pad), z(NOPE), t2, t1, z(pad)], axis=-1)

    w_uq4 = w_uq.reshape(depth, Q_RANK, N_HEADS, QK_DIM)
    w_uq_pad = jnp.pad(w_uq4, ((0, 0), (0, 0), (0, 0), (0, pad))).reshape(depth, Q_RANK, N_HEADS * HEAD_PAD)
    w_ukv4 = w_ukv.reshape(depth, KV_RANK, N_HEADS, NOPE + V_DIM)
    w_uk = jnp.pad(w_ukv4[..., :NOPE], ((0, 0), (0, 0), (0, 0), (0, HEAD_PAD - NOPE)))
    w_uv = w_ukv4[..., NOPE:].reshape(depth, KV_RANK, N_HEADS * V_DIM)

    zg = lambda n: jnp.zeros((depth, n), F32)
    q_scale = (QK_DIM ** -0.5) * LOG2E
    qg = jnp.concatenate([q_head_g * q_scale, zg(pad)], axis=-1)[..., None]
    score_bound = (QK_DIM ** 0.5) * LOG2E * jnp.max(jnp.abs(q_head_g), axis=-1) * jnp.max(jnp.abs(k_head_g), axis=-1)
    bounded = score_bound <= MAX_SCORE_BOUND
    off = jnp.where(bounded, -score_bound, 0.0)
    qoff = jnp.concatenate([zg(QK_DIM), off[:, None], zg(pad - 1)], axis=-1)[..., None]
    g_nope, g1, g2 = k_head_g[:, :NOPE], k_head_g[:, NOPE:NOPE + half], k_head_g[:, NOPE + half:]
    gn = jnp.concatenate([g_nope, zg(HEAD_PAD - NOPE)], axis=-1)[:, None, :]
    ga = jnp.concatenate([zg(NOPE), g1, g2, zg(pad)], axis=-1)[:, None, :]
    gb = jnp.concatenate([zg(NOPE), g2, g1, zg(pad)], axis=-1)[:, None, :]
    return dict(
        norm_g=norm_g[:, None, :],
        w1=w1.astype(BF16),
        q_norm_g=q_norm_g[:, None, :],
        kv_norm_g=kv_norm_g[:, None, :],
        w_uqT=jnp.swapaxes(w_uq_pad, 1, 2).astype(BF16),
        w_uk=w_uk.reshape(depth, KV_RANK, N_HEADS * HEAD_PAD).astype(BF16),
        w_uvT=jnp.swapaxes(w_uv, 1, 2).astype(BF16),
        qg=qg, qoff=qoff, bounded=bounded, gn=gn, ga=ga, gb=gb,
        w_rest=w_in[:, :, OFF_REST:].astype(BF16),
        b_gate=b_gate[:, None, :],
        conv_w=conv_w,
        conv_b=conv_b[:, None, :],
        mem_q_g=mem_q_g[:, None, :],
        w_br_attn=w_br_attn.astype(BF16),
        w_br_conv=w_br_conv.astype(BF16),
        w_br_mem=w_br_mem.astype(BF16),
        w_out=w_out.astype(BF16),
    )


def kernel(x, mem, positions, norm_g, w_in, b_gate, q_norm_g, w_uq, kv_norm_g, w_ukv, q_head_g, k_head_g,
           conv_w, conv_b, mem_norm_g, w_mkv, mem_q_g, mem_k_g, w_br_attn, w_br_conv, w_br_mem, w_out):
    depth = w_in.shape[0]
    assert x.shape[-1] == D_MODEL and w_in.shape[-1] == OFF_REST + REST_COLS
    assert x.shape[1] % max(TS_PRE, TS_POST, TQB, TS_TAB) == 0
    p = _prepare_params(norm_g, w_in, b_gate, q_norm_g, w_uq, kv_norm_g, w_ukv, q_head_g, k_head_g,
                        conv_w, conv_b, mem_q_g, w_br_attn, w_br_conv, w_br_mem, w_out)
    invf = ROPE_BASE ** (-jnp.arange(0, ROPE, 2, dtype=F32) / ROPE)
    tabs = _rope_tables(positions, invf[:, None])
    mk, mv = _mem_kv(mem, mem_norm_g, w_mkv.astype(BF16), mem_k_g)

    for i in range(depth):
        layer = jnp.full((1,), i, jnp.int32)
        qT, k, vT = _pre_call(layer, x, p, tabs)
        oT = lax.cond(p["bounded"][i], _attn_bounded_call, _attn_rowmax_call, qT, k, vT)
        x = _post_call(layer, x, oT, mk, mv, p)
    return x
```

```python
import math

import jax
import jax.numpy as jnp
from jax import lax
from jax.experimental import pallas as pl
from jax.experimental.pallas import tpu as pltpu

D_MODEL = 1024
N_HEADS = 8
NOPE = 64
ROPE = 32
QK_DIM = NOPE + ROPE
V_DIM = 64
Q_RANK = 3 * D_MODEL // 8
KV_RANK = D_MODEL // 4
MLA_W = N_HEADS * V_DIM
CONV_W = D_MODEL // 2
MEM_HEADS = 4
MEM_HD = 128
MEM_W = MEM_HEADS * MEM_HD
N_BRANCH = 3
ROPE_BASE = 10000.0
EPS = 1e-6
LOG2E = math.log2(math.e)

LANES = 128
SUBLANES = 8
HEAD_PAD = LANES
VMEM_LIMIT = 56 * 1024 * 1024

OFF_KVLAT = Q_RANK
OFF_KPE = Q_RANK + KV_RANK
OFF_REST = OFF_KPE + ROPE
W1_COLS = OFF_KPE + 2 * HEAD_PAD
R_CB, R_CC, R_CU, R_QM, R_GA, R_GC, R_GM, R_R = (0, 512, 1024, 1536, 2048, 2560, 3072, 3584)
REST_COLS = R_R + N_BRANCH * D_MODEL

TS_TAB = 512
TS_PRE = 1024
TS_PRE_SUB = 256
TS_POST = 512
HALO = SUBLANES
PROJ_ROWS = 256
TQB = 1024
TQH = 512
TKB = 256
SHIFT_ROWS = 2
MAX_SCORE_BOUND = 48.0

F32 = jnp.float32
BF16 = jnp.bfloat16
NT_DIMS = (((1,), (1,)), ((), ()))


def _rms(t):
    return t * lax.rsqrt(jnp.mean(t * t, axis=-1, keepdims=True) + EPS)


def _sigmoid(t):
    return 0.5 * jnp.tanh(0.5 * t) + 0.5


def _silu(t):
    return t * _sigmoid(t)


def _tables_kernel(pos_ref, invf_ref, cosT_ref, sinT_ref, cosF_ref, sinF_ref):
    pos = pos_ref[0].astype(F32)
    ang = invf_ref[...] * pos
    c = jnp.cos(ang)
    s = jnp.sin(ang)
    cosT_ref[0] = c
    sinT_ref[0] = s
    t = pos.shape[1]
    z64 = jnp.zeros((NOPE, t), F32)
    z32 = jnp.zeros((HEAD_PAD - QK_DIM, t), F32)
    cosF_ref[0] = jnp.concatenate([z64, c, c, z32], axis=0).T
    sinF_ref[0] = jnp.concatenate([z64, -s, s, z32], axis=0).T


def _rope_tables(positions, invf):
    b, s = positions.shape
    half = ROPE // 2
    return pl.pallas_call(
        _tables_kernel,
        grid=(b, s // TS_TAB),
        in_specs=[pl.BlockSpec((1, 1, TS_TAB), lambda i, j: (i, 0, j)),
                  pl.BlockSpec((half, 1), lambda i, j: (0, 0))],
        out_specs=[pl.BlockSpec((1, half, TS_TAB), lambda i, j: (i, 0, j)),
                   pl.BlockSpec((1, half, TS_TAB), lambda i, j: (i, 0, j)),
                   pl.BlockSpec((1, TS_TAB, HEAD_PAD), lambda i, j: (i, j, 0)),
                   pl.BlockSpec((1, TS_TAB, HEAD_PAD), lambda i, j: (i, j, 0))],
        out_shape=[jax.ShapeDtypeStruct((b, half, s), F32),
                   jax.ShapeDtypeStruct((b, half, s), F32),
                   jax.ShapeDtypeStruct((b, s, HEAD_PAD), F32),
                   jax.ShapeDtypeStruct((b, s, HEAD_PAD), F32)],
        compiler_params=pltpu.CompilerParams(dimension_semantics=("parallel", "parallel")),
        name="rope_tables",
    )(positions.reshape(b, 1, s), invf)


def _mem_kernel(mem_ref, g_ref, w_ref, kg_ref, mk_ref, mv_ref):
    m = mem_ref[0]
    mn = (_rms(m) * g_ref[0]).astype(BF16)
    mkv = jnp.dot(mn, w_ref[0], preferred_element_type=F32)
    for h in range(MEM_HEADS):
        kh = mkv[:, h * 2 * MEM_HD: h * 2 * MEM_HD + MEM_HD]
        vh = mkv[:, h * 2 * MEM_HD + MEM_HD: (h + 1) * 2 * MEM_HD]
        mk_ref[0, 0, h] = (_rms(kh) * kg_ref[0]).astype(BF16)
        mv_ref[0, 0, h] = vh.astype(BF16)


def _mem_kv(mem, mem_norm_g, w_mkv_bf, mem_k_g):
    depth = w_mkv_bf.shape[0]
    b, m, d = mem.shape
    return pl.pallas_call(
        _mem_kernel,
        grid=(depth, b),
        in_specs=[pl.BlockSpec((1, m, d), lambda l, i: (i, 0, 0)),
                  pl.BlockSpec((1, 1, d), lambda l, i: (l, 0, 0)),
                  pl.BlockSpec((1, d, 2 * MEM_W), lambda l, i: (l, 0, 0)),
                  pl.BlockSpec((1, 1, MEM_HD), lambda l, i: (l, 0, 0))],
        out_specs=[pl.BlockSpec((1, 1, MEM_HEADS, m, MEM_HD), lambda l, i: (l, i, 0, 0, 0)),
                   pl.BlockSpec((1, 1, MEM_HEADS, m, MEM_HD), lambda l, i: (l, i, 0, 0, 0))],
        out_shape=[jax.ShapeDtypeStruct((depth, b, MEM_HEADS, m, MEM_HD), BF16),
                   jax.ShapeDtypeStruct((depth, b, MEM_HEADS, m, MEM_HD), BF16)],
        compiler_params=pltpu.CompilerParams(dimension_semantics=("parallel", "parallel")),
        name="mem_kv",
    )(mem, mem_norm_g.reshape(depth, 1, d), w_mkv_bf, mem_k_g.reshape(depth, 1, MEM_HD))


def _pre_kernel(layer_ref, x_ref, ng_ref, w1_ref, qng_ref, kvng_ref, wuq_ref, wuk_ref, wuv_ref,
                qg_ref, qoff_ref, gn_ref, ga_ref, gb_ref, cosT_ref, sinT_ref, cosF_ref, sinF_ref,
                qT_ref, k_ref, vT_ref):
    del layer_ref
    half = ROPE // 2
    qg = qg_ref[0]
    qoff = qoff_ref[0]
    gn = gn_ref[0]
    lane = lax.broadcasted_iota(jnp.int32, (1, HEAD_PAD), 1)
    one_lanes = ((lane >= QK_DIM) & (lane < QK_DIM + SHIFT_ROWS)).astype(F32)
    subs = [slice(i * TS_PRE_SUB, (i + 1) * TS_PRE_SUB) for i in range(x_ref.shape[1] // TS_PRE_SUB)]

    a_subs = []
    for r in subs:
        hb = (_rms(x_ref[0, r, :]) * ng_ref[0]).astype(BF16)
        a_subs.append(jnp.dot(hb, w1_ref[0], preferred_element_type=F32))

    for r, a in zip(subs, a_subs):
        qn = (_rms(a[:, :Q_RANK]) * qng_ref[0]).astype(BF16)
        kvn = (_rms(a[:, OFF_KVLAT:OFF_KPE]) * kvng_ref[0]).astype(BF16)
        pe = a[:, OFF_KPE:OFF_KPE + HEAD_PAD]
        pe_sw = a[:, OFF_KPE + HEAD_PAD:W1_COLS]
        qT = lax.dot_general(wuq_ref[0], qn, NT_DIMS, preferred_element_type=F32)
        knope = jnp.dot(kvn, wuk_ref[0], preferred_element_type=F32)
        vT = lax.dot_general(wuv_ref[0], kvn, NT_DIMS, preferred_element_type=F32)

        cT = cosT_ref[0, :, r]
        sT = sinT_ref[0, :, r]
        for h in range(N_HEADS):
            qh = qT[h * HEAD_PAD:(h + 1) * HEAD_PAD]
            ms = jnp.sum(qh * qh, axis=0, keepdims=True) * (1.0 / QK_DIM)
            qh = qh * lax.rsqrt(ms + EPS) * qg + qoff
            t1 = qh[NOPE:NOPE + half]
            t2 = qh[NOPE + half:QK_DIM]
            qh = jnp.concatenate([qh[:NOPE], t1 * cT - t2 * sT, t2 * cT + t1 * sT, qh[QK_DIM:]], axis=0)
            qT_ref[0, h, :, r] = qh.astype(BF16)

        u = pe * (ga_ref[0] * cosF_ref[0, r, :]) + pe_sw * (gb_ref[0] * sinF_ref[0, r, :])
        ss_pe = jnp.sum(pe * pe, axis=-1, keepdims=True)
        for h in range(N_HEADS):
            kn = knope[:, h * HEAD_PAD:(h + 1) * HEAD_PAD]
            ms = (jnp.sum(kn * kn, axis=-1, keepdims=True) + ss_pe) * (1.0 / QK_DIM)
            k_ref[0, h, r, :] = ((kn * gn + u) * lax.rsqrt(ms + EPS) + one_lanes).astype(BF16)

        vT_ref[0, :, :, r] = vT.reshape(N_HEADS, V_DIM, vT.shape[-1]).astype(BF16)


def _pre_call(layer, x, p, tabs):
    b, s, d = x.shape
    cosT, sinT, cosF, sinF = tabs
    half = ROPE // 2
    wspec = lambda shape: pl.BlockSpec((1,) + shape, lambda i, j, l: (l[0],) + (0,) * len(shape))
    grid_spec = pltpu.PrefetchScalarGridSpec(
        num_scalar_prefetch=1,
        grid=(b, s // TS_PRE),
        in_specs=[pl.BlockSpec((1, TS_PRE, d), lambda i, j, l: (i, j, 0)),
                  wspec((1, d)), wspec((d, W1_COLS)), wspec((1, Q_RANK)), wspec((1, KV_RANK)),
                  wspec((N_HEADS * HEAD_PAD, Q_RANK)), wspec((KV_RANK, N_HEADS * HEAD_PAD)),
                  wspec((N_HEADS * V_DIM, KV_RANK)),
                  wspec((HEAD_PAD, 1)), wspec((HEAD_PAD, 1)),
                  wspec((1, HEAD_PAD)), wspec((1, HEAD_PAD)), wspec((1, HEAD_PAD)),
                  pl.BlockSpec((1, half, TS_PRE), lambda i, j, l: (i, 0, j)),
                  pl.BlockSpec((1, half, TS_PRE), lambda i, j, l: (i, 0, j)),
                  pl.BlockSpec((1, TS_PRE, HEAD_PAD), lambda i, j, l: (i, j, 0)),
                  pl.BlockSpec((1, TS_PRE, HEAD_PAD), lambda i, j, l: (i, j, 0))],
        out_specs=[pl.BlockSpec((1, N_HEADS, HEAD_PAD, TS_PRE), lambda i, j, l: (i, 0, 0, j)),
                   pl.BlockSpec((1, N_HEADS, TS_PRE, HEAD_PAD), lambda i, j, l: (i, 0, j, 0)),
                   pl.BlockSpec((1, N_HEADS, V_DIM, TS_PRE), lambda i, j, l: (i, 0, 0, j))],
    )
    return pl.pallas_call(
        _pre_kernel,
        grid_spec=grid_spec,
        out_shape=[jax.ShapeDtypeStruct((b, N_HEADS, HEAD_PAD, s), BF16),
                   jax.ShapeDtypeStruct((b, N_HEADS, s, HEAD_PAD), BF16),
                   jax.ShapeDtypeStruct((b, N_HEADS, V_DIM, s), BF16)],
        compiler_params=pltpu.CompilerParams(dimension_semantics=("parallel", "parallel"),
                                             vmem_limit_bytes=VMEM_LIMIT),
        name="pre_attn",
    )(layer, x, p["norm_g"], p["w1"], p["q_norm_g"], p["kv_norm_g"], p["w_uqT"], p["w_uk"], p["w_uvT"],
      p["qg"], p["qoff"], p["gn"], p["ga"], p["gb"], cosT, sinT, cosF, sinF)


def _rowmax_kernel(qT_ref, k_ref, m_ref):
    n_chunks = k_ref.shape[2] // TKB
    n_tiles = qT_ref.shape[3] // TQH

    def query_tile(j, carry):
        q0 = pl.multiple_of(j * TQH, TQH)
        qT = qT_ref[0, 0, :, pl.ds(q0, TQH)]
        m8 = jnp.full((SUBLANES, TQH), -jnp.inf, F32)
        for c in range(n_chunks):
            s = jnp.dot(k_ref[0, 0, c * TKB:(c + 1) * TKB, :], qT, preferred_element_type=F32)
            m8 = jnp.maximum(m8, jnp.max(s.reshape(TKB // SUBLANES, SUBLANES, TQH), axis=0))
        m_ref[0, 0, :, pl.ds(q0, TQH)] = jnp.max(m8, axis=0, keepdims=True)
        return carry

    lax.fori_loop(0, n_tiles, query_tile, 0)


def _rowmax_call(qT, k):
    b, nh, _, s = qT.shape
    assert s % TKB == 0 and s % TQH == 0
    return pl.pallas_call(
        _rowmax_kernel,
        grid=(b, nh),
        in_specs=[pl.BlockSpec((1, 1, HEAD_PAD, s), lambda i, h: (i, h, 0, 0)),
                  pl.BlockSpec((1, 1, s, HEAD_PAD), lambda i, h: (i, h, 0, 0))],
        out_specs=pl.BlockSpec((1, 1, 1, s), lambda i, h: (i, h, 0, 0)),
        out_shape=jax.ShapeDtypeStruct((b, nh, 1, s), F32),
        compiler_params=pltpu.CompilerParams(dimension_semantics=("parallel", "parallel"),
                                             vmem_limit_bytes=VMEM_LIMIT),
        name="mla_rowmax",
    )(qT, k)


def _attn_bounded_kernel(qT_ref, k_ref, vT_ref, oT_ref):
    n_chunks = k_ref.shape[2] // TKB
    n_tiles = qT_ref.shape[3] // TQB

    def query_tile(j, carry):
        q0 = pl.multiple_of(j * TQB, TQB)
        qT = qT_ref[0, 0, :, pl.ds(q0, TQB)]
        halves = [qT[:, h * TQH:(h + 1) * TQH] for h in range(TQB // TQH)]

        def scores(c, h):
            return jnp.dot(k_ref[0, 0, c * TKB:(c + 1) * TKB, :], halves[h], preferred_element_type=F32)

        n_h = len(halves)
        l8 = [jnp.zeros((SUBLANES, TQH), F32) for _ in range(n_h)]
        acc = [jnp.zeros((V_DIM, TQH), F32) for _ in range(n_h)]
        s_next = [scores(0, h) for h in range(n_h)]
        for c in range(n_chunks):
            vc = vT_ref[0, 0, :, c * TKB:(c + 1) * TKB]
            for h in range(n_h):
                s_cur = s_next[h]
                if c + 1 < n_chunks:
                    s_next[h] = scores(c + 1, h)
                p = jnp.exp2(s_cur)
                l8[h] = l8[h] + jnp.sum(p.reshape(TKB // SUBLANES, SUBLANES, TQH), axis=0)
                acc[h] = acc[h] + jnp.dot(vc, p.astype(BF16), preferred_element_type=F32)
        for h in range(n_h):
            oT_ref[0, 0, :, pl.ds(q0 + h * TQH, TQH)] = acc[h] / jnp.sum(l8[h], axis=0, keepdims=True)
        return carry

    lax.fori_loop(0, n_tiles, query_tile, 0)


def _attn_bounded_call(qT, k, vT):
    b, nh, _, s = qT.shape
    assert s % TKB == 0 and s % TQB == 0
    return pl.pallas_call(
        _attn_bounded_kernel,
        grid=(b, nh),
        in_specs=[pl.BlockSpec((1, 1, HEAD_PAD, s), lambda i, h: (i, h, 0, 0)),
                  pl.BlockSpec((1, 1, s, HEAD_PAD), lambda i, h: (i, h, 0, 0)),
                  pl.BlockSpec((1, 1, V_DIM, s), lambda i, h: (i, h, 0, 0))],
        out_specs=pl.BlockSpec((1, 1, V_DIM, s), lambda i, h: (i, h, 0, 0)),
        out_shape=jax.ShapeDtypeStruct((b, nh, V_DIM, s), F32),
        compiler_params=pltpu.CompilerParams(dimension_semantics=("parallel", "parallel"),
                                             vmem_limit_bytes=VMEM_LIMIT),
        name="mla_attn_bounded",
    )(qT, k, vT)


def _attn_rowmax_call(qT, k, vT):
    m = _rowmax_call(qT, k)
    hi = (-m).astype(BF16)
    lo = (-m - hi.astype(F32)).astype(BF16)
    qT = lax.dynamic_update_slice(qT, jnp.concatenate([hi, lo], axis=2), (0, 0, QK_DIM, 0))
    return _attn_bounded_call(qT, k, vT)


def _post_kernel(layer_ref, x_ref, xp_ref, xn_ref, oT_ref, mk_ref, mv_ref, ng_ref, wr_ref, bg_ref,
                 cw_ref, cb_ref, mqg_ref, wa_ref, wc_ref, wm_ref, wo_ref, out_ref):
    del layer_ref
    j = pl.program_id(1)
    nj = pl.num_programs(1)
    x = x_ref[0]
    t = x.shape[0]
    ng = ng_ref[0]
    hb = (_rms(x) * ng).astype(BF16)
    xh = jnp.concatenate([xp_ref[0, 0], xn_ref[0, 0]], axis=0)
    hh = (_rms(xh) * ng).astype(BF16)
    h_ext = jnp.concatenate([hb, hh], axis=0)

    def proj(lhs, lo, hi):
        m = lhs.shape[0]
        if m % PROJ_ROWS:
            return jnp.dot(lhs, wr_ref[0, :, lo:hi], preferred_element_type=F32)
        parts = [jnp.dot(lhs[r:r + PROJ_ROWS], wr_ref[0, :, lo:hi], preferred_element_type=F32)
                 for r in range(0, m, PROJ_ROWS)]
        return jnp.concatenate(parts, axis=0)

    ccu = proj(h_ext, R_CC, R_QM)
    z_ext = ccu[:, :CONV_W] * ccu[:, CONV_W:]
    z = z_ext[:t]
    z_before = z_ext[t + HALO - 1:t + HALO] * (j > 0).astype(F32)
    z_after = z_ext[t + HALO:t + HALO + 1] * (j < nj - 1).astype(F32)
    row = lax.broadcasted_iota(jnp.int32, z.shape, 0)
    z_prev = jnp.where(row == 0, z_before, pltpu.roll(z, 1, 0))
    z_next = jnp.where(row == t - 1, z_after, pltpu.roll(z, t - 1, 0))
    cw = cw_ref[0]
    conv = z_prev * cw[0:1] + z * cw[1:2] + z_next * cw[2:3] + cb_ref[0]
    o_conv = proj(hb, R_CB, R_CC) * conv * _silu(proj(hb, R_GC, R_GM))
    y_conv = jnp.dot(o_conv.astype(BF16), wc_ref[0], preferred_element_type=F32)

    oT = oT_ref[0]
    o_attn = oT.reshape(MLA_W, t).T * _silu(proj(hb, R_GA, R_GC))
    y_attn = jnp.dot(o_attn.astype(BF16), wa_ref[0], preferred_element_type=F32)

    qm = proj(hb, R_QM, R_GA)
    mqg = mqg_ref[0]
    heads = []
    for h in range(MEM_HEADS):
        qh = (_rms(qm[:, h * MEM_HD:(h + 1) * MEM_HD]) * mqg).astype(BF16)
        s = lax.dot_general(qh, mk_ref[0, 0, h], NT_DIMS, preferred_element_type=F32)
        s = s * (MEM_HD ** -0.5)
        p = jnp.exp(s - jnp.max(s, axis=-1, keepdims=True))
        l = jnp.sum(p, axis=-1, keepdims=True)
        oh = jnp.dot(p.astype(BF16), mv_ref[0, 0, h], preferred_element_type=F32)
        heads.append(oh / l)
    o_mem = jnp.concatenate(heads, axis=-1) * _silu(proj(hb, R_GM, R_R))
    y_mem = jnp.dot(o_mem.astype(BF16), wm_ref[0], preferred_element_type=F32)

    bg = bg_ref[0]
    r_a = _sigmoid(proj(hb, R_R, R_R + D_MODEL) + bg[:, :D_MODEL])
    y = r_a * y_attn
    r_c = _sigmoid(proj(hb, R_R + D_MODEL, R_R + 2 * D_MODEL) + bg[:, D_MODEL:2 * D_MODEL])
    y = y + r_c * y_conv
    r_m = _sigmoid(proj(hb, R_R + 2 * D_MODEL, R_R + 3 * D_MODEL) + bg[:, 2 * D_MODEL:])
    y = y + r_m * y_mem
    out_ref[0] = x + jnp.dot(y.astype(BF16), wo_ref[0], preferred_element_type=F32)


def _post_call(layer, x, oT, mk, mv, p):
    b, s, d = x.shape
    m = mk.shape[3]
    nblk = TS_POST // HALO
    last = s // HALO - 1
    x4 = x.reshape(b, s // HALO, HALO, d)
    wspec = lambda shape: pl.BlockSpec((1,) + shape, lambda i, j, l: (l[0],) + (0,) * len(shape),
                                       pipeline_mode=pl.Buffered(1))
    grid_spec = pltpu.PrefetchScalarGridSpec(
        num_scalar_prefetch=1,
        grid=(b, s // TS_POST),
        in_specs=[pl.BlockSpec((1, TS_POST, d), lambda i, j, l: (i, j, 0)),
                  pl.BlockSpec((1, 1, HALO, d), lambda i, j, l: (i, jnp.maximum(j * nblk - 1, 0), 0, 0)),
                  pl.BlockSpec((1, 1, HALO, d), lambda i, j, l: (i, jnp.minimum((j + 1) * nblk, last), 0, 0)),
                  pl.BlockSpec((1, N_HEADS, V_DIM, TS_POST), lambda i, j, l: (i, 0, 0, j)),
                  pl.BlockSpec((1, 1, MEM_HEADS, m, MEM_HD), lambda i, j, l: (l[0], i, 0, 0, 0)),
                  pl.BlockSpec((1, 1, MEM_HEADS, m, MEM_HD), lambda i, j, l: (l[0], i, 0, 0, 0)),
                  wspec((1, d)), wspec((d, REST_COLS)), wspec((1, N_BRANCH * d)),
                  wspec((3, CONV_W)), wspec((1, CONV_W)), wspec((1, MEM_HD)),
                  wspec((MLA_W, d)), wspec((CONV_W, d)), wspec((MEM_W, d)), wspec((d, d))],
        out_specs=pl.BlockSpec((1, TS_POST, d), lambda i, j, l: (i, j, 0)),
    )
    return pl.pallas_call(
        _post_kernel,
        grid_spec=grid_spec,
        out_shape=jax.ShapeDtypeStruct((b, s, d), F32),
        compiler_params=pltpu.CompilerParams(dimension_semantics=("parallel", "parallel"),
                                             vmem_limit_bytes=VMEM_LIMIT),
        name="post_attn",
    )(layer, x, x4, x4, oT, mk, mv, p["norm_g"], p["w_rest"], p["b_gate"], p["conv_w"], p["conv_b"],
      p["mem_q_g"], p["w_br_attn"], p["w_br_conv"], p["w_br_mem"], p["w_out"])


def _prepare_params(norm_g, w_in, b_gate, q_norm_g, w_uq, kv_norm_g, w_ukv, q_head_g, k_head_g,
                    conv_w, conv_b, mem_q_g, w_br_attn, w_br_conv, w_br_mem, w_out):
    depth = w_in.shape[0]
    d = D_MODEL
    half = ROPE // 2
    z = lambda n: jnp.zeros((depth, d, n), F32)
    t1 = w_in[:, :, OFF_KPE:OFF_KPE + half]
    t2 = w_in[:, :, OFF_KPE + half:OFF_REST]
    pad = HEAD_PAD - QK_DIM
    w1 = jnp.concatenate([w_in[:, :, :OFF_KPE], z(NOPE), t1, t2, z(pad), z(NOPE), t2, t1, z(pad)], axis=-1)

    w_uq4 = w_uq.reshape(depth, Q_RANK, N_HEADS, QK_DIM)
    w_uq_pad = jnp.pad(w_uq4, ((0, 0), (0, 0), (0, 0), (0, pad))).reshape(depth, Q_RANK, N_HEADS * HEAD_PAD)
    w_ukv4 = w_ukv.reshape(depth, KV_RANK, N_HEADS, NOPE + V_DIM)
    w_uk = jnp.pad(w_ukv4[..., :NOPE], ((0, 0), (0, 0), (0, 0), (0, HEAD_PAD - NOPE)))
    w_uv = w_ukv4[..., NOPE:].reshape(depth, KV_RANK, N_HEADS * V_DIM)

    zg = lambda n: jnp.zeros((depth, n), F32)
    q_scale = (QK_DIM ** -0.5) * LOG2E
    qg = jnp.concatenate([q_head_g * q_scale, zg(pad)], axis=-1)[..., None]
    score_bound = (QK_DIM ** 0.5) * LOG2E * jnp.max(jnp.abs(q_head_g), axis=-1) * jnp.max(jnp.abs(k_head_g), axis=-1)
    bounded = score_bound <= MAX_SCORE_BOUND
    off = jnp.where(bounded, -score_bound, 0.0)
    qoff = jnp.concatenate([zg(QK_DIM), off[:, None], zg(pad - 1)], axis=-1)[..., None]
    g_nope, g1, g2 = k_head_g[:, :NOPE], k_head_g[:, NOPE:NOPE + half], k_head_g[:, NOPE + half:]
    gn = jnp.concatenate([g_nope, zg(HEAD_PAD - NOPE)], axis=-1)[:, None, :]
    ga = jnp.concatenate([zg(NOPE), g1, g2, zg(pad)], axis=-1)[:, None, :]
    gb = jnp.concatenate([zg(NOPE), g2, g1, zg(pad)], axis=-1)[:, None, :]
    return dict(
        norm_g=norm_g[:, None, :],
        w1=w1.astype(BF16),
        q_norm_g=q_norm_g[:, None, :],
        kv_norm_g=kv_norm_g[:, None, :],
        w_uqT=jnp.swapaxes(w_uq_pad, 1, 2).astype(BF16),
        w_uk=w_uk.reshape(depth, KV_RANK, N_HEADS * HEAD_PAD).astype(BF16),
        w_uvT=jnp.swapaxes(w_uv, 1, 2).astype(BF16),
        qg=qg, qoff=qoff, bounded=bounded, gn=gn, ga=ga, gb=gb,
        w_rest=w_in[:, :, OFF_REST:].astype(BF16),
        b_gate=b_gate[:, None, :],
        conv_w=conv_w,
        conv_b=conv_b[:, None, :],
        mem_q_g=mem_q_g[:, None, :],
        w_br_attn=w_br_attn.astype(BF16),
        w_br_conv=w_br_conv.astype(BF16),
        w_br_mem=w_br_mem.astype(BF16),
        w_out=w_out.astype(BF16),
    )


def kernel(x, mem, positions, norm_g, w_in, b_gate, q_norm_g, w_uq, kv_norm_g, w_ukv, q_head_g, k_head_g,
           conv_w, conv_b, mem_norm_g, w_mkv, mem_q_g, mem_k_g, w_br_attn, w_br_conv, w_br_mem, w_out):
    depth = w_in.shape[0]
    assert x.shape[-1] == D_MODEL and w_in.shape[-1] == OFF_REST + REST_COLS
    assert x.shape[1] % max(TS_PRE, TS_POST, TQB, TS_TAB) == 0
    p = _prepare_params(norm_g, w_in, b_gate, q_norm_g, w_uq, kv_norm_g, w_ukv, q_head_g, k_head_g,
                        conv_w, conv_b, mem_q_g, w_br_attn, w_br_conv, w_br_mem, w_out)
    invf = ROPE_BASE ** (-jnp.arange(0, ROPE, 2, dtype=F32) / ROPE)
    tabs = _rope_tables(positions, invf[:, None])
    mk, mv = _mem_kv(mem, mem_norm_g, w_mkv.astype(BF16), mem_k_g)

    for i in range(depth):
        layer = jnp.full((1,), i, jnp.int32)
        qT, k, vT = _pre_call(layer, x, p, tabs)
        oT = lax.cond(p["bounded"][i], _attn_bounded_call, _attn_rowmax_call, qT, k, vT)
        x = _post_call(layer, x, oT, mk, mv, p)
    return x
```
